```python
import jax, jax.numpy as jnp
from jax import lax
import numpy as np

D_MODEL = 1024
BATCH = 16
SEQ = 4096
DEPTH = 4
DEC_BATCH = 2
DEC_SEQ = 8192
PAST_LEN = 128

N_HEADS = 16
N_KV_HEADS = 4
HEAD_DIM = D_MODEL // N_HEADS
GROUP = N_HEADS // N_KV_HEADS
QKV_DIM = (N_HEADS + 2 * N_KV_HEADS) * HEAD_DIM
D_FF = 2816
WINDOW = 128
BLOCK = 128
GRID_W = 64
ROPE_THETA = 10000.0
N_MIXERS = 2
N_A = (DEPTH + 1) // 2
N_B = DEPTH // 2
N_NORMS = 6
EPS = 1e-6

kernel_name = "hybrid_window_axial_gqa_encoder"


def rms_norm(x, g):
    xf = x.astype(jnp.float32)
    y = xf * lax.rsqrt(jnp.mean(xf * xf, axis=-1, keepdims=True) + EPS)
    return (y * g.astype(jnp.float32)).astype(x.dtype)


def rope(x, pos):
    d = x.shape[-1]
    half = d // 2
    inv_freq = ROPE_THETA ** (-jnp.arange(half, dtype=jnp.float32) / half)
    ang = pos[:, None] * inv_freq[None, :]
    cos = jnp.cos(ang)[:, None, :]
    sin = jnp.sin(ang)[:, None, :]
    xf = x.astype(jnp.float32)
    x1, x2 = xf[..., :half], xf[..., half:]
    out = jnp.concatenate([x1 * cos - x2 * sin, x2 * cos + x1 * sin], axis=-1)
    return out.astype(x.dtype)


def swiglu(x, w_gate, w_up, w_down):
    return (jax.nn.silu(x @ w_gate) * (x @ w_up)) @ w_down


def split_qkv(h, w_qkv):
    B, T, _ = h.shape
    qkv = h @ w_qkv
    q = qkv[..., :N_HEADS * HEAD_DIM].reshape(B, T, N_HEADS, HEAD_DIM)
    k = qkv[..., N_HEADS * HEAD_DIM:(N_HEADS + N_KV_HEADS) * HEAD_DIM].reshape(B, T, N_KV_HEADS, HEAD_DIM)
    v = qkv[..., (N_HEADS + N_KV_HEADS) * HEAD_DIM:].reshape(B, T, N_KV_HEADS, HEAD_DIM)
    return q, k, v


def query_blocks(q):
    B, T = q.shape[:2]
    nb = T // BLOCK
    qb = q.reshape(B, nb, BLOCK, N_KV_HEADS, GROUP, HEAD_DIM).transpose(1, 0, 2, 3, 4, 5)
    return qb.astype(jnp.float32) * (HEAD_DIM ** -0.5)


def merge_blocks(o, B, T):
    return o.transpose(1, 0, 2, 3, 4, 5).reshape(B, T, N_HEADS * HEAD_DIM)


def windowed_sink_attention(q, k, v, sink):
    B, T = q.shape[:2]
    nb = T // BLOCK
    qb = query_blocks(q)
    pad = ((0, 0), (WINDOW, WINDOW), (0, 0), (0, 0))
    kpad = jnp.pad(k, pad)
    vpad = jnp.pad(v, pad)
    span = BLOCK + 2 * WINDOW
    r = jnp.arange(BLOCK)
    j = jnp.arange(span)
    band = jnp.abs(j[None, :] - WINDOW - r[:, None]) <= WINDOW
    sink_f = sink.astype(jnp.float32).reshape(N_KV_HEADS, GROUP)[None, :, :, None, None]

    def step(args):
        qblk, b = args
        start = b * BLOCK
        kb = lax.dynamic_slice_in_dim(kpad, start, span, axis=1).astype(jnp.float32)
        vb = lax.dynamic_slice_in_dim(vpad, start, span, axis=1).astype(jnp.float32)
        kpos = start - WINDOW + j
        valid = band & ((kpos >= 0) & (kpos < T))[None, :]
        s = jnp.einsum('bqkgd,bskd->bkgqs', qblk, kb)
        s = jnp.where(valid, s, -jnp.inf)
        m = jnp.maximum(jnp.max(s, axis=-1, keepdims=True), sink_f)
        p = jnp.exp(s - m)
        denom = jnp.sum(p, axis=-1, keepdims=True) + jnp.exp(sink_f - m)
        return jnp.einsum('bkgqs,bskd->bqkgd', p / denom, vb)

    o = lax.map(step, (qb, jnp.arange(nb)))
    return merge_blocks(o, B, T).astype(q.dtype)


def dense_attention(q, k, v):
    B, T = q.shape[:2]
    qb = query_blocks(q)
    kf = k.astype(jnp.float32)
    vf = v.astype(jnp.float32)

    def step(qblk):
        s = jnp.einsum('bqkgd,bskd->bkgqs', qblk, kf)
        p = jax.nn.softmax(s, axis=-1)
        return jnp.einsum('bkgqs,bskd->bqkgd', p, vf)

    o = lax.map(step, qb)
    return merge_blocks(o, B, T).astype(q.dtype)


def mixer_window(h, w_qkv, w_o, sink, pos):
    q, k, v = split_qkv(h, w_qkv)
    q = rope(q, pos)
    k = rope(k, pos)
    return windowed_sink_attention(q, k, v, sink) @ w_o


def axial_rope(x, row, col):
    half = HEAD_DIM // 2
    return jnp.concatenate([rope(x[..., :half], row), rope(x[..., half:], col)], axis=-1)


def mixer_axial(h, w_qkv, w_o, q_gain, k_gain, row, col):
    q, k, v = split_qkv(h, w_qkv)
    q = axial_rope(rms_norm(q, q_gain), row, col)
    k = axial_rope(rms_norm(k, k_gain), row, col)
    return dense_attention(q, k, v) @ w_o


def encoder(x, norm_g, w_qkv, w_o, attn_sink, q_norm, k_norm, w_gate, w_up, w_down):
    T = x.shape[1]
    rows = T // GRID_W
    pos = jnp.arange(T, dtype=jnp.float32)
    row = jnp.repeat(jnp.arange(rows, dtype=jnp.float32), GRID_W)
    col = jnp.tile(jnp.arange(GRID_W, dtype=jnp.float32), rows)
    for i in range(DEPTH):
        g = norm_g[i]
        x = x + 0.5 * rms_norm(swiglu(rms_norm(x, g[0]), w_gate[i, 0], w_up[i, 0], w_down[i, 0]), g[1])
        h = rms_norm(x, g[2])
        if i % N_MIXERS == 0:
            m = mixer_window(h, w_qkv[i], w_o[i], attn_sink[i // N_MIXERS], pos)
        else:
            m = mixer_axial(h, w_qkv[i], w_o[i], q_norm[i // N_MIXERS], k_norm[i // N_MIXERS], row, col)
        x = x + rms_norm(m, g[3])
        x = x + 0.5 * rms_norm(swiglu(rms_norm(x, g[4]), w_gate[i, 1], w_up[i, 1], w_down[i, 1]), g[5])
    return x


def setup_inputs(seed: int = 0) -> dict:
    key = jax.random.key(seed)
    ks = jax.random.split(key, 11)
    f32 = jnp.float32
    x_prompt = jax.random.normal(ks[0], (BATCH, SEQ, D_MODEL), f32)
    x_sample = jax.random.normal(ks[1], (DEC_BATCH, DEC_SEQ, D_MODEL), f32)
    norm_g = 1.0 + 0.02 * jax.random.normal(ks[2], (DEPTH, N_NORMS, D_MODEL), f32)
    w_qkv = jax.random.normal(ks[3], (DEPTH, D_MODEL, QKV_DIM), f32) * D_MODEL ** -0.5
    w_o = jax.random.normal(ks[4], (DEPTH, N_HEADS * HEAD_DIM, D_MODEL), f32) * (N_HEADS * HEAD_DIM) ** -0.5
    attn_sink = jax.random.normal(ks[5], (N_A, N_HEADS), f32)
    q_norm = 1.0 + 0.02 * jax.random.normal(ks[6], (N_B, HEAD_DIM), f32)
    k_norm = 1.0 + 0.02 * jax.random.normal(ks[7], (N_B, HEAD_DIM), f32)
    w_gate = jax.random.normal(ks[8], (DEPTH, 2, D_MODEL, D_FF), f32) * D_MODEL ** -0.5
    w_up = jax.random.normal(ks[9], (DEPTH, 2, D_MODEL, D_FF), f32) * D_MODEL ** -0.5
    w_down = jax.random.normal(ks[10], (DEPTH, 2, D_FF, D_MODEL), f32) * D_FF ** -0.5
    return {"x_prompt": x_prompt, "x_sample": x_sample, "norm_g": norm_g, "w_qkv": w_qkv, "w_o": w_o,
            "attn_sink": attn_sink, "q_norm": q_norm, "k_norm": k_norm,
            "w_gate": w_gate, "w_up": w_up, "w_down": w_down}


def reference(x_prompt, x_sample, norm_g, w_qkv, w_o, attn_sink, q_norm, k_norm, w_gate, w_up, w_down):
    y_prompt = encoder(x_prompt, norm_g, w_qkv, w_o, attn_sink, q_norm, k_norm, w_gate, w_up, w_down)
    y_sample = encoder(x_sample, norm_g, w_qkv, w_o, attn_sink, q_norm, k_norm, w_gate, w_up, w_down)
    return (y_prompt, y_sample)
```

```python
import functools

import jax
import jax.numpy as jnp
from jax import lax
from jax.experimental import pallas as pl
from jax.experimental.pallas import tpu as pltpu

D_MODEL = 1024
DEPTH = 4
N_HEADS = 16
N_KV_HEADS = 4
HEAD_DIM = D_MODEL // N_HEADS
GROUP = N_HEADS // N_KV_HEADS
Q_DIM = N_HEADS * HEAD_DIM
KV_DIM = N_KV_HEADS * HEAD_DIM
QKV_DIM = Q_DIM + 2 * KV_DIM
D_FF = 2816
WINDOW = 128
GRID_W = 64
ROPE_THETA = 10000.0
EPS = 1e-6

BF16 = jnp.bfloat16
F32 = jnp.float32

MASKED_SCORE = -1e30
VMEM_LIMIT_BYTES = 56 * 1024 * 1024

TM_FFN = 512
TM_QKV = 512
TQ_DENSE = 256
TK_DENSE = 512
TQ_WIN = 256


def _rms(x, g):
    return (x * lax.rsqrt(jnp.mean(x * x, axis=-1, keepdims=True) + EPS)) * g


def _resident(shape):
    return pl.BlockSpec(shape, lambda *_: (0,) * len(shape), pipeline_mode=pl.Buffered(1))


def _params(*semantics):
    return pltpu.CompilerParams(dimension_semantics=semantics, vmem_limit_bytes=VMEM_LIMIT_BYTES)


def _ffn_kernel(x_ref, gpre_ref, gpost_ref, wg_ref, wu_ref, wd_ref, o_ref):
    x = x_ref[...]
    h = _rms(x, gpre_ref[...]).astype(BF16)
    gate = jnp.dot(h, wg_ref[...], preferred_element_type=F32)
    up = jnp.dot(h, wu_ref[...], preferred_element_type=F32)
    act = (gate * jax.nn.sigmoid(gate) * up).astype(BF16)
    y = jnp.dot(act, wd_ref[...], preferred_element_type=F32)
    o_ref[...] = x + 0.5 * _rms(y, gpost_ref[...])


def _ffn(x, g_pre, g_post, wg, wu, wd):
    n = x.shape[0]
    tm = TM_FFN
    row = pl.BlockSpec((tm, D_MODEL), lambda i: (i, 0))
    return pl.pallas_call(
        _ffn_kernel,
        grid=(n // tm,),
        in_specs=[row, _resident((1, D_MODEL)), _resident((1, D_MODEL)),
                  _resident((D_MODEL, D_FF)), _resident((D_MODEL, D_FF)), _resident((D_FF, D_MODEL))],
        out_specs=row,
        out_shape=jax.ShapeDtypeStruct((n, D_MODEL), F32),
        compiler_params=_params("parallel"),
        name="ffn",
    )(x, g_pre, g_post, wg, wu, wd)


def _rotate_half(x, axial):
    if axial:
        q = HEAD_DIM // 4
        parts = [x[:, q:2 * q], x[:, 0:q], x[:, 3 * q:4 * q], x[:, 2 * q:3 * q]]
    else:
        h = HEAD_DIM // 2
        parts = [x[:, h:], x[:, :h]]
    return jnp.concatenate(parts, axis=1)


def _head_post(xt, gain, cos, sin, axial):
    if axial:
        ms = jnp.mean(xt * xt, axis=1, keepdims=True)
        xt = (xt * lax.rsqrt(ms + EPS)) * gain[None]
    return xt * cos[None] + _rotate_half(xt, axial) * sin[None]


def _qkv_kernel(x_ref, g_ref, w_ref, cos_ref, sin_ref, qg_ref, kg_ref,
                qt_ref, k_ref, vt_ref, *, axial, vch):
    tm = x_ref.shape[1]
    h = _rms(x_ref[0], g_ref[...]).astype(BF16)
    qkv = jnp.dot(h, w_ref[...], preferred_element_type=F32)
    cos = cos_ref[...]
    sin = sin_ref[...]

    qt = qkv[:, :Q_DIM].T.reshape(N_HEADS, HEAD_DIM, tm)
    qt = _head_post(qt, qg_ref[...], cos, sin, axial) * (HEAD_DIM ** -0.5)
    qt_ref[0] = qt.reshape(Q_DIM, tm).astype(BF16)

    kt = qkv[:, Q_DIM:Q_DIM + KV_DIM].T.reshape(N_KV_HEADS, HEAD_DIM, tm)
    kt = _head_post(kt, kg_ref[...], cos, sin, axial)
    k_ref[0] = kt.reshape(KV_DIM, tm).T.astype(BF16)

    v = qkv[:, Q_DIM + KV_DIM:]
    for j in range(tm // vch):
        vt_ref[0, j] = v[j * vch:(j + 1) * vch, :].T.astype(BF16)


def _qkv(x, g, w, cos_t, sin_t, q_gain, k_gain, *, axial, vch):
    b, t, _ = x.shape
    tm = TM_QKV
    kern = functools.partial(_qkv_kernel, axial=axial, vch=vch)
    return pl.pallas_call(
        kern,
        grid=(b, t // tm),
        in_specs=[pl.BlockSpec((1, tm, D_MODEL), lambda bi, i: (bi, i, 0)),
                  _resident((1, D_MODEL)),
                  _resident((D_MODEL, QKV_DIM)),
                  pl.BlockSpec((HEAD_DIM, tm), lambda bi, i: (0, i)),
                  pl.BlockSpec((HEAD_DIM, tm), lambda bi, i: (0, i)),
                  _resident((HEAD_DIM, 1)), _resident((HEAD_DIM, 1))],
        out_specs=[pl.BlockSpec((1, Q_DIM, tm), lambda bi, i: (bi, 0, i)),
                   pl.BlockSpec((1, tm, KV_DIM), lambda bi, i: (bi, i, 0)),
                   pl.BlockSpec((1, tm // vch, KV_DIM, vch), lambda bi, i: (bi, i, 0, 0))],
        out_shape=[jax.ShapeDtypeStruct((b, Q_DIM, t), BF16),
                   jax.ShapeDtypeStruct((b, t, KV_DIM), BF16),
                   jax.ShapeDtypeStruct((b, t // vch, KV_DIM, vch), BF16)],
        compiler_params=_params("parallel", "parallel"),
        name="qkv_axial" if axial else "qkv_window",
    )(x, g, w, cos_t, sin_t, q_gain, k_gain)


def _fill_padded_queries(qpad_ref, qt_ref, kh, tq):
    qpad_ref[...] = jnp.zeros(qpad_ref.shape, qpad_ref.dtype)
    for g in range(GROUP):
        h = kh * GROUP + g
        qpad_ref[kh * HEAD_DIM:(kh + 1) * HEAD_DIM, g * tq:(g + 1) * tq] = (
            qt_ref[0, h * HEAD_DIM:(h + 1) * HEAD_DIM, :])


def _store_heads(o_ref, ot, kh, tq):
    stacked = jnp.concatenate([ot[:, g * tq:(g + 1) * tq] for g in range(GROUP)], axis=0)
    width = GROUP * HEAD_DIM
    o_ref[0, :, kh * width:(kh + 1) * width] = stacked.T.astype(o_ref.dtype)


def _dense_attn_kernel(qt_ref, k_ref, vt_ref, o_ref, qpad_ref, *, tq, tk):
    t = k_ref.shape[1]
    n = GROUP * tq
    for kh in range(N_KV_HEADS):
        _fill_padded_queries(qpad_ref, qt_ref, kh, tq)

        def chunk(c, carry, kh=kh):
            m, l, acc = carry
            off = pl.multiple_of(c * tk, tk)
            s = jnp.dot(k_ref[0, pl.ds(off, tk), :], qpad_ref[...],
                        preferred_element_type=F32)
            m_new = jnp.maximum(m, jnp.max(s, axis=0, keepdims=True))
            alpha = jnp.exp(m - m_new)
            p = jnp.exp(s - m_new)
            l = alpha * l + jnp.sum(p, axis=0, keepdims=True)
            vc = vt_ref[0, c, kh * HEAD_DIM:(kh + 1) * HEAD_DIM, :]
            acc = alpha * acc + jnp.dot(vc, p.astype(BF16), preferred_element_type=F32)
            return m_new, l, acc

        init = (jnp.full((1, n), -jnp.inf, F32), jnp.zeros((1, n), F32),
                jnp.zeros((HEAD_DIM, n), F32))
        _, l, acc = lax.fori_loop(0, t // tk, chunk, init)
        _store_heads(o_ref, acc * (1.0 / l), kh, tq)


def _dense_attention(qt, k, vt):
    b, _, t = qt.shape
    tq, tk = TQ_DENSE, TK_DENSE
    kern = functools.partial(_dense_attn_kernel, tq=tq, tk=tk)
    return pl.pallas_call(
        kern,
        grid=(b, t // tq),
        in_specs=[pl.BlockSpec((1, Q_DIM, tq), lambda bi, i: (bi, 0, i)),
                  pl.BlockSpec((1, t, KV_DIM), lambda bi, i: (bi, 0, 0)),
                  pl.BlockSpec((1, t // tk, KV_DIM, tk), lambda bi, i: (bi, 0, 0, 0))],
        out_specs=pl.BlockSpec((1, tq, Q_DIM), lambda bi, i: (bi, i, 0)),
        out_shape=jax.ShapeDtypeStruct((b, t, Q_DIM), BF16),
        scratch_shapes=[pltpu.VMEM((KV_DIM, GROUP * tq), BF16)],
        compiler_params=_params("parallel", "arbitrary"),
        name="attn_dense",
    )(qt, k, vt)


def _window_attn_kernel(qt_ref, kp_ref, kc_ref, kn_ref, vp_ref, vc_ref, vn_ref,
                        bias_ref, sink_ref, o_ref, qpad_ref, *, tq):
    i = pl.program_id(1)
    last = pl.num_programs(1) - 1
    w = WINDOW
    keys = jnp.concatenate([kp_ref[0], kc_ref[0], kn_ref[0]], axis=0)
    bias = jnp.concatenate([
        jnp.where(i > 0, bias_ref[:w], MASKED_SCORE),
        bias_ref[w:w + tq],
        jnp.where(i < last, bias_ref[w + tq:], MASKED_SCORE)], axis=0)
    for kh in range(N_KV_HEADS):
        _fill_padded_queries(qpad_ref, qt_ref, kh, tq)
        rows = slice(kh * HEAD_DIM, (kh + 1) * HEAD_DIM)
        vals = jnp.concatenate([vp_ref[0, 0, rows, :]]
                               + [vc_ref[0, j, rows, :] for j in range(tq // w)]
                               + [vn_ref[0, 0, rows, :]], axis=1)
        s = jnp.dot(keys, qpad_ref[...], preferred_element_type=F32) + bias
        sink = sink_ref[kh]
        m = jnp.maximum(jnp.max(s, axis=0, keepdims=True), sink)
        p = jnp.exp(s - m)
        denom = jnp.sum(p, axis=0, keepdims=True) + jnp.exp(sink - m)
        ot = jnp.dot(vals, p.astype(BF16), preferred_element_type=F32)
        _store_heads(o_ref, ot * (1.0 / denom), kh, tq)


def _window_bias(tq):
    span = tq + 2 * WINDOW
    key_rel = jnp.arange(span)[:, None] - WINDOW
    q_rel = (jnp.arange(GROUP * tq) % tq)[None, :]
    return jnp.where(jnp.abs(key_rel - q_rel) <= WINDOW, 0.0, MASKED_SCORE).astype(F32)


def _window_attention(qt, k, vt, sink):
    b, _, t = qt.shape
    tq, w = TQ_WIN, WINDOW
    r = tq // w
    nblk = t // w
    n = GROUP * tq
    span = tq + 2 * w
    sink_rows = jnp.repeat(sink.astype(F32).reshape(N_KV_HEADS, 1, GROUP), tq, axis=2)
    prev = lambda i: jnp.maximum(i * r - 1, 0)
    nxt = lambda i: jnp.minimum(i * r + r, nblk - 1)
    kern = functools.partial(_window_attn_kernel, tq=tq)
    return pl.pallas_call(
        kern,
        grid=(b, t // tq),
        in_specs=[pl.BlockSpec((1, Q_DIM, tq), lambda bi, i: (bi, 0, i)),
                  pl.BlockSpec((1, w, KV_DIM), lambda bi, i: (bi, prev(i), 0)),
                  pl.BlockSpec((1, tq, KV_DIM), lambda bi, i: (bi, i, 0)),
                  pl.BlockSpec((1, w, KV_DIM), lambda bi, i: (bi, nxt(i), 0)),
                  pl.BlockSpec((1, 1, KV_DIM, w), lambda bi, i: (bi, prev(i), 0, 0)),
                  pl.BlockSpec((1, r, KV_DIM, w), lambda bi, i: (bi, i, 0, 0)),
                  pl.BlockSpec((1, 1, KV_DIM, w), lambda bi, i: (bi, nxt(i), 0, 0)),
                  _resident((span, n)),
                  _resident((N_KV_HEADS, 1, n))],
        out_specs=pl.BlockSpec((1, tq, Q_DIM), lambda bi, i: (bi, i, 0)),
        out_shape=jax.ShapeDtypeStruct((b, t, Q_DIM), BF16),
        scratch_shapes=[pltpu.VMEM((KV_DIM, n), BF16)],
        compiler_params=_params("parallel", "arbitrary"),
        name="attn_window",
    )(qt, k, k, k, vt, vt, vt, _window_bias(tq), sink_rows)


def _oproj_kernel(a_ref, x_ref, g_ref, w_ref, o_ref):
    y = jnp.dot(a_ref[...], w_ref[...], preferred_element_type=F32)
    o_ref[...] = x_ref[...] + _rms(y, g_ref[...])


def _oproj(a, x, g, w):
    n = x.shape[0]
    tm = TM_FFN
    row = pl.BlockSpec((tm, D_MODEL), lambda i: (i, 0))
    return pl.pallas_call(
        _oproj_kernel,
        grid=(n // tm,),
        in_specs=[row, row, _resident((1, D_MODEL)), _resident((Q_DIM, D_MODEL))],
        out_specs=row,
        out_shape=jax.ShapeDtypeStruct((n, D_MODEL), F32),
        compiler_params=_params("parallel"),
        name="oproj",
    )(a, x, g, w)


def _signed_tables(ang):
    cos = jnp.cos(ang)
    sin = jnp.sin(ang)
    return jnp.concatenate([cos, cos], axis=1), jnp.concatenate([-sin, sin], axis=1)


def _window_tables(t):
    half = HEAD_DIM // 2
    inv_freq = ROPE_THETA ** (-jnp.arange(half, dtype=F32) / half)
    pos = jnp.arange(t, dtype=F32)
    cos, sin = _signed_tables(pos[:, None] * inv_freq[None, :])
    return cos.T, sin.T


def _axial_tables(t):
    half = HEAD_DIM // 4
    rows = t // GRID_W
    inv_freq = ROPE_THETA ** (-jnp.arange(half, dtype=F32) / half)
    row = jnp.repeat(jnp.arange(rows, dtype=F32), GRID_W)
    col = jnp.tile(jnp.arange(GRID_W, dtype=F32), rows)
    cos_r, sin_r = _signed_tables(row[:, None] * inv_freq[None, :])
    cos_c, sin_c = _signed_tables(col[:, None] * inv_freq[None, :])
    return (jnp.concatenate([cos_r, cos_c], axis=1).T,
            jnp.concatenate([sin_r, sin_c], axis=1).T)


def _encoder(x, norm_g, w_qkv, w_o, attn_sink, q_norm, k_norm, w_gate, w_up, w_down):
    b, t, d = x.shape
    tables = (_window_tables(t), _axial_tables(t))
    ones = jnp.ones((HEAD_DIM, 1), F32)
    x = x.reshape(b * t, d)
    for i in range(DEPTH):
        g = norm_g[i].reshape(6, 1, D_MODEL)
        x = _ffn(x, g[0], g[1], w_gate[i, 0], w_up[i, 0], w_down[i, 0])
        axial = i % 2 == 1
        cos_t, sin_t = tables[i % 2]
        if axial:
            qg = q_norm[i // 2].reshape(HEAD_DIM, 1)
            kg = k_norm[i // 2].reshape(HEAD_DIM, 1)
            qt, k, vt = _qkv(x.reshape(b, t, d), g[2], w_qkv[i], cos_t, sin_t, qg, kg,
                             axial=True, vch=TK_DENSE)
            a = _dense_attention(qt, k, vt)
        else:
            qt, k, vt = _qkv(x.reshape(b, t, d), g[2], w_qkv[i], cos_t, sin_t, ones, ones,
                             axial=False, vch=WINDOW)
            a = _window_attention(qt, k, vt, attn_sink[i // 2])
        x = _oproj(a.reshape(b * t, Q_DIM), x, g[3], w_o[i])
        x = _ffn(x, g[4], g[5], w_gate[i, 1], w_up[i, 1], w_down[i, 1])
    return x.reshape(b, t, d)


def kernel(x_prompt, x_sample, norm_g, w_qkv, w_o, attn_sink, q_norm, k_norm, w_gate, w_up, w_down):
    weights = (norm_g, w_qkv.astype(BF16), w_o.astype(BF16), attn_sink, q_norm, k_norm,
               w_gate.astype(BF16), w_up.astype(BF16), w_down.astype(BF16))
    return (_encoder(x_prompt, *weights), _encoder(x_sample, *weights))
```

```python
import functools

import jax
import jax.numpy as jnp
from jax import lax
from jax.experimental import pallas as pl
from jax.experimental.pallas import tpu as pltpu

D_MODEL = 1024
DEPTH = 4
N_HEADS = 16
N_KV_HEADS = 4
HEAD_DIM = D_MODEL // N_HEADS
GROUP = N_HEADS // N_KV_HEADS
Q_DIM = N_HEADS * HEAD_DIM
KV_DIM = N_KV_HEADS * HEAD_DIM
QKV_DIM = Q_DIM + 2 * KV_DIM
D_FF = 2816
WINDOW = 128
GRID_W = 64
ROPE_THETA = 10000.0
EPS = 1e-6

BF16 = jnp.bfloat16
F32 = jnp.float32

MASKED_SCORE = -1e30
LOG2E = 1.4426950408889634
VMEM_LIMIT_BYTES = 56 * 1024 * 1024

K_AUG = 128
V_AUG = 80
ONES_AT = HEAD_DIM
BOUND_SLACK = 1.01
STATIC_BOUND_MAX = 46.0

TM_FFN = 512
TM_QKV = 512
TQ_DENSE = 256
TK_DENSE = 512
TQ_WIN = 256


def _rms(x, g):
    return (x * lax.rsqrt(jnp.mean(x * x, axis=-1, keepdims=True) + EPS)) * g


def _resident(shape):
    return pl.BlockSpec(shape, lambda *_: (0,) * len(shape), pipeline_mode=pl.Buffered(1))


def _params(*semantics):
    return pltpu.CompilerParams(dimension_semantics=semantics, vmem_limit_bytes=VMEM_LIMIT_BYTES)


def _ffn_kernel(x_ref, gpre_ref, gpost_ref, wg_ref, wu_ref, wd_ref, o_ref):
    x = x_ref[...]
    h = _rms(x, gpre_ref[...]).astype(BF16)
    gate = jnp.dot(h, wg_ref[...], preferred_element_type=F32)
    up = jnp.dot(h, wu_ref[...], preferred_element_type=F32)
    act = (gate * jax.nn.sigmoid(gate) * up).astype(BF16)
    y = jnp.dot(act, wd_ref[...], preferred_element_type=F32)
    o_ref[...] = x + 0.5 * _rms(y, gpost_ref[...])


def _ffn(x, g_pre, g_post, wg, wu, wd):
    n = x.shape[0]
    tm = TM_FFN
    row = pl.BlockSpec((tm, D_MODEL), lambda i: (i, 0))
    return pl.pallas_call(
        _ffn_kernel,
        grid=(n // tm,),
        in_specs=[row, _resident((1, D_MODEL)), _resident((1, D_MODEL)),
                  _resident((D_MODEL, D_FF)), _resident((D_MODEL, D_FF)), _resident((D_FF, D_MODEL))],
        out_specs=row,
        out_shape=jax.ShapeDtypeStruct((n, D_MODEL), F32),
        compiler_params=_params("parallel"),
        name="ffn",
    )(x, g_pre, g_post, wg, wu, wd)


def _rotate_half(x, axial):
    if axial:
        q = HEAD_DIM // 4
        parts = [x[:, q:2 * q], x[:, 0:q], x[:, 3 * q:4 * q], x[:, 2 * q:3 * q]]
    else:
        h = HEAD_DIM // 2
        parts = [x[:, h:], x[:, :h]]
    return jnp.concatenate(parts, axis=1)


def _head_post(xt, gain, cos, sin, axial):
    if axial:
        ms = jnp.mean(xt * xt, axis=1, keepdims=True)
        xt = (xt * lax.rsqrt(ms + EPS)) * gain[None]
    return xt * cos[None] + _rotate_half(xt, axial) * sin[None]


def _with_ones(xt, rows):
    heads, _, n = xt.shape
    pad_shape = (heads, rows - HEAD_DIM, n)
    pad = (lax.broadcasted_iota(jnp.int32, pad_shape, 1) == 0).astype(xt.dtype)
    return jnp.concatenate([xt, pad], axis=1).reshape(heads * rows, n)


def _qkv_kernel(x_ref, g_ref, w_ref, cos_ref, sin_ref, qg_ref, kg_ref,
                qt_ref, k_ref, vt_ref, *, axial, vch):
    tm = x_ref.shape[1]
    h = _rms(x_ref[0], g_ref[...]).astype(BF16)
    qkv = jnp.dot(h, w_ref[...], preferred_element_type=F32)
    cos = cos_ref[...]
    sin = sin_ref[...]

    q_scale = HEAD_DIM ** -0.5 * (LOG2E if axial else 1.0)
    qt = qkv[:, :Q_DIM].T.reshape(N_HEADS, HEAD_DIM, tm)
    qt = _head_post(qt, qg_ref[...], cos, sin, axial) * q_scale
    qt_ref[0] = qt.reshape(Q_DIM, tm).astype(BF16)

    kt = qkv[:, Q_DIM:Q_DIM + KV_DIM].T.reshape(N_KV_HEADS, HEAD_DIM, tm)
    kt = _head_post(kt, kg_ref[...], cos, sin, axial)
    kt = _with_ones(kt, K_AUG) if axial else kt.reshape(KV_DIM, tm)
    k_ref[0] = kt.T.astype(BF16)

    v = qkv[:, Q_DIM + KV_DIM:]
    for j in range(tm // vch):
        vt = v[j * vch:(j + 1) * vch, :].T
        if axial:
            vt = _with_ones(vt.reshape(N_KV_HEADS, HEAD_DIM, vch), V_AUG)
        vt_ref[0, j] = vt.astype(BF16)


def _qkv(x, g, w, cos_t, sin_t, q_gain, k_gain, *, axial, vch):
    b, t, _ = x.shape
    tm = TM_QKV
    k_width = N_KV_HEADS * K_AUG if axial else KV_DIM
    v_rows = N_KV_HEADS * V_AUG if axial else KV_DIM
    kern = functools.partial(_qkv_kernel, axial=axial, vch=vch)
    return pl.pallas_call(
        kern,
        grid=(b, t // tm),
        in_specs=[pl.BlockSpec((1, tm, D_MODEL), lambda bi, i: (bi, i, 0)),
                  _resident((1, D_MODEL)),
                  _resident((D_MODEL, QKV_DIM)),
                  pl.BlockSpec((HEAD_DIM, tm), lambda bi, i: (0, i)),
                  pl.BlockSpec((HEAD_DIM, tm), lambda bi, i: (0, i)),
                  _resident((HEAD_DIM, 1)), _resident((HEAD_DIM, 1))],
        out_specs=[pl.BlockSpec((1, Q_DIM, tm), lambda bi, i: (bi, 0, i)),
                   pl.BlockSpec((1, tm, k_width), lambda bi, i: (bi, i, 0)),
                   pl.BlockSpec((1, tm // vch, v_rows, vch), lambda bi, i: (bi, i, 0, 0))],
        out_shape=[jax.ShapeDtypeStruct((b, Q_DIM, t), BF16),
                   jax.ShapeDtypeStruct((b, t, k_width), BF16),
                   jax.ShapeDtypeStruct((b, t // vch, v_rows, vch), BF16)],
        compiler_params=_params("parallel", "parallel"),
        name="qkv_axial" if axial else "qkv_window",
    )(x, g, w, cos_t, sin_t, q_gain, k_gain)


def _store_heads(o_ref, ot, kh, tq):
    stacked = jnp.concatenate([ot[:, g * tq:(g + 1) * tq] for g in range(GROUP)], axis=0)
    width = GROUP * HEAD_DIM
    o_ref[0, :, kh * width:(kh + 1) * width] = stacked.T.astype(o_ref.dtype)


def _dense_attn_kernel(qt_ref, k_ref, vt_ref, o_ref, qaug_ref, sa_ref, sb_ref, acc_ref, knorm_ref,
                       *, tq, tk):
    t = k_ref.shape[1]
    n = GROUP * tq
    nc = t // tk

    @pl.when(pl.program_id(1) == 0)
    def _():
        for kh in range(N_KV_HEADS):
            def norm_chunk(c, best, kh=kh):
                off = pl.multiple_of(c * tk, tk)
                kf = k_ref[0, pl.ds(off, tk), kh * K_AUG:(kh + 1) * K_AUG].astype(F32)
                return jnp.maximum(best, jnp.sum(kf * kf, axis=1, keepdims=True))
            best = lax.fori_loop(0, nc, norm_chunk, jnp.zeros((tk, 1), F32))
            knorm_ref[kh] = jnp.sqrt(jnp.max(best))

    for kh in range(N_KV_HEADS):
        kcols = slice(kh * K_AUG, (kh + 1) * K_AUG)
        vrows = slice(kh * V_AUG, (kh + 1) * V_AUG)

        qnorm = []
        for g in range(GROUP):
            h = kh * GROUP + g
            qh = qt_ref[0, h * HEAD_DIM:(h + 1) * HEAD_DIM, :]
            qaug_ref[0:HEAD_DIM, g * tq:(g + 1) * tq] = qh
            qf = qh.astype(F32)
            qnorm.append(jnp.sqrt(jnp.sum(qf * qf, axis=0, keepdims=True)))
        bound = jnp.concatenate(qnorm, axis=1) * (knorm_ref[kh] * BOUND_SLACK)
        static_ok = jnp.max(bound) <= STATIC_BOUND_MAX
        offset = jnp.where(static_ok, -bound, 0.0)
        tail_rows = lax.broadcasted_iota(jnp.int32, (K_AUG - HEAD_DIM, n), 0)
        qaug_ref[HEAD_DIM:, :] = jnp.where(tail_rows == 0, offset, 0.0).astype(BF16)
        acc_ref[...] = jnp.zeros(acc_ref.shape, F32)

        def scores(c, dst, kcols=kcols):
            off = pl.multiple_of(c * tk, tk)
            dst[...] = jnp.dot(k_ref[0, pl.ds(off, tk), kcols], qaug_ref[...],
                               preferred_element_type=F32)

        def consume(c, src, vrows=vrows):
            p = jnp.exp2(src[...]).astype(BF16)
            acc_ref[...] += jnp.dot(vt_ref[0, c, vrows, :], p, preferred_element_type=F32)

        @pl.when(static_ok)
        def _():
            scores(0, sa_ref)

            def pair(c2, carry):
                c = 2 * c2
                scores(c + 1, sb_ref)
                consume(c, sa_ref)
                scores(c + 2, sa_ref)
                consume(c + 1, sb_ref)
                return carry

            lax.fori_loop(0, nc // 2 - 1, pair, 0)
            scores(nc - 1, sb_ref)
            consume(nc - 2, sa_ref)
            consume(nc - 1, sb_ref)

        @pl.when(jnp.logical_not(static_ok))
        def _():
            def chunk(c, m, vrows=vrows):
                scores(c, sa_ref)
                s = sa_ref[...]
                m_new = jnp.maximum(m, jnp.max(s, axis=0, keepdims=True))
                p = jnp.exp2(s - m_new).astype(BF16)
                acc_ref[...] = (jnp.exp2(m - m_new) * acc_ref[...]
                                + jnp.dot(vt_ref[0, c, vrows, :], p, preferred_element_type=F32))
                return m_new

            lax.fori_loop(0, nc, chunk, jnp.full((1, n), -jnp.inf, F32))

        acc = acc_ref[...]
        _store_heads(o_ref, acc[:HEAD_DIM] * (1.0 / acc[ONES_AT:ONES_AT + 1]), kh, tq)


def _dense_attention(qt, k, vt):
    b, _, t = qt.shape
    tq, tk = TQ_DENSE, TK_DENSE
    n = GROUP * tq
    kern = functools.partial(_dense_attn_kernel, tq=tq, tk=tk)
    return pl.pallas_call(
        kern,
        grid=(b, t // tq),
        in_specs=[pl.BlockSpec((1, Q_DIM, tq), lambda bi, i: (bi, 0, i)),
                  pl.BlockSpec((1, t, N_KV_HEADS * K_AUG), lambda bi, i: (bi, 0, 0)),
                  pl.BlockSpec((1, t // tk, N_KV_HEADS * V_AUG, tk), lambda bi, i: (bi, 0, 0, 0))],
        out_specs=pl.BlockSpec((1, tq, Q_DIM), lambda bi, i: (bi, i, 0)),
        out_shape=jax.ShapeDtypeStruct((b, t, Q_DIM), BF16),
        scratch_shapes=[pltpu.VMEM((K_AUG, n), BF16),
                        pltpu.VMEM((tk, n), F32), pltpu.VMEM((tk, n), F32),
                        pltpu.VMEM((V_AUG, n), F32),
                        pltpu.SMEM((N_KV_HEADS,), F32)],
        compiler_params=_params("parallel", "arbitrary"),
        name="attn_dense",
    )(qt, k, vt)


def _fill_padded_queries(qpad_ref, qt_ref, kh, tq):
    qpad_ref[...] = jnp.zeros(qpad_ref.shape, qpad_ref.dtype)
    for g in range(GROUP):
        h = kh * GROUP + g
        qpad_ref[kh * HEAD_DIM:(kh + 1) * HEAD_DIM, g * tq:(g + 1) * tq] = (
            qt_ref[0, h * HEAD_DIM:(h + 1) * HEAD_DIM, :])


def _window_attn_kernel(qt_ref, kp_ref, kc_ref, kn_ref, vp_ref, vc_ref, vn_ref,
                        bias_ref, sink_ref, o_ref, qpad_ref, *, tq):
    i = pl.program_id(1)
    last = pl.num_programs(1) - 1
    w = WINDOW
    keys = jnp.concatenate([kp_ref[0], kc_ref[0], kn_ref[0]], axis=0)
    bias = jnp.concatenate([
        jnp.where(i > 0, bias_ref[:w], MASKED_SCORE),
        bias_ref[w:w + tq],
        jnp.where(i < last, bias_ref[w + tq:], MASKED_SCORE)], axis=0)
    for kh in range(N_KV_HEADS):
        _fill_padded_queries(qpad_ref, qt_ref, kh, tq)
        rows = slice(kh * HEAD_DIM, (kh + 1) * HEAD_DIM)
        vals = jnp.concatenate([vp_ref[0, 0, rows, :]]
                               + [vc_ref[0, j, rows, :] for j in range(tq // w)]
                               + [vn_ref[0, 0, rows, :]], axis=1)
        s = jnp.dot(keys, qpad_ref[...], preferred_element_type=F32) + bias
        sink = sink_ref[kh]
        m = jnp.maximum(jnp.max(s, axis=0, keepdims=True), sink)
        p = jnp.exp(s - m)
        denom = jnp.sum(p, axis=0, keepdims=True) + jnp.exp(sink - m)
        ot = jnp.dot(vals, p.astype(BF16), preferred_element_type=F32)
        _store_heads(o_ref, ot * (1.0 / denom), kh, tq)


def _window_bias(tq):
    span = tq + 2 * WINDOW
    key_rel = jnp.arange(span)[:, None] - WINDOW
    q_rel = (jnp.arange(GROUP * tq) % tq)[None, :]
    return jnp.where(jnp.abs(key_rel - q_rel) <= WINDOW, 0.0, MASKED_SCORE).astype(F32)


def _window_attention(qt, k, vt, sink):
    b, _, t = qt.shape
    tq, w = TQ_WIN, WINDOW
    r = tq // w
    nblk = t // w
    n = GROUP * tq
    span = tq + 2 * w
    sink_rows = jnp.repeat(sink.astype(F32).reshape(N_KV_HEADS, 1, GROUP), tq, axis=2)
    prev = lambda i: jnp.maximum(i * r - 1, 0)
    nxt = lambda i: jnp.minimum(i * r + r, nblk - 1)
    kern = functools.partial(_window_attn_kernel, tq=tq)
    return pl.pallas_call(
        kern,
        grid=(b, t // tq),
        in_specs=[pl.BlockSpec((1, Q_DIM, tq), lambda bi, i: (bi, 0, i)),
                  pl.BlockSpec((1, w, KV_DIM), lambda bi, i: (bi, prev(i), 0)),
                  pl.BlockSpec((1, tq, KV_DIM), lambda bi, i: (bi, i, 0)),
                  pl.BlockSpec((1, w, KV_DIM), lambda bi, i: (bi, nxt(i), 0)),
                  pl.BlockSpec((1, 1, KV_DIM, w), lambda bi, i: (bi, prev(i), 0, 0)),
                  pl.BlockSpec((1, r, KV_DIM, w), lambda bi, i: (bi, i, 0, 0)),
                  pl.BlockSpec((1, 1, KV_DIM, w), lambda bi, i: (bi, nxt(i), 0, 0)),
                  _resident((span, n)),
                  _resident((N_KV_HEADS, 1, n))],
        out_specs=pl.BlockSpec((1, tq, Q_DIM), lambda bi, i: (bi, i, 0)),
        out_shape=jax.ShapeDtypeStruct((b, t, Q_DIM), BF16),
        scratch_shapes=[pltpu.VMEM((KV_DIM, n), BF16)],
        compiler_params=_params("parallel", "arbitrary"),
        name="attn_window",
    )(qt, k, k, k, vt, vt, vt, _window_bias(tq), sink_rows)


def _oproj_kernel(a_ref, x_ref, g_ref, w_ref, o_ref):
    y = jnp.dot(a_ref[...], w_ref[...], preferred_element_type=F32)
    o_ref[...] = x_ref[...] + _rms(y, g_ref[...])


def _oproj(a, x, g, w):
    n = x.shape[0]
    tm = TM_FFN
    row = pl.BlockSpec((tm, D_MODEL), lambda i: (i, 0))
    return pl.pallas_call(
        _oproj_kernel,
        grid=(n // tm,),
        in_specs=[row, row, _resident((1, D_MODEL)), _resident((Q_DIM, D_MODEL))],
        out_specs=row,
        out_shape=jax.ShapeDtypeStruct((n, D_MODEL), F32),
        compiler_params=_params("parallel"),
        name="oproj",
    )(a, x, g, w)


def _signed_tables(ang):
    cos = jnp.cos(ang)
    sin = jnp.sin(ang)
    return jnp.concatenate([cos, cos], axis=1), jnp.concatenate([-sin, sin], axis=1)


def _window_tables(t):
    half = HEAD_DIM // 2
    inv_freq = ROPE_THETA ** (-jnp.arange(half, dtype=F32) / half)
    pos = jnp.arange(t, dtype=F32)
    cos, sin = _signed_tables(pos[:, None] * inv_freq[None, :])
    return cos.T, sin.T


def _axial_tables(t):
    half = HEAD_DIM // 4
    rows = t // GRID_W
    inv_freq = ROPE_THETA ** (-jnp.arange(half, dtype=F32) / half)
    row = jnp.repeat(jnp.arange(rows, dtype=F32), GRID_W)
    col = jnp.tile(jnp.arange(GRID_W, dtype=F32), rows)
    cos_r, sin_r = _signed_tables(row[:, None] * inv_freq[None, :])
    cos_c, sin_c = _signed_tables(col[:, None] * inv_freq[None, :])
    return (jnp.concatenate([cos_r, cos_c], axis=1).T,
            jnp.concatenate([sin_r, sin_c], axis=1).T)


def _encoder(x, norm_g, w_qkv, w_o, attn_sink, q_norm, k_norm, w_gate, w_up, w_down):
    b, t, d = x.shape
    tables = (_window_tables(t), _axial_tables(t))
    ones = jnp.ones((HEAD_DIM, 1), F32)
    x = x.reshape(b * t, d)
    for i in range(DEPTH):
        g = norm_g[i].reshape(6, 1, D_MODEL)
        x = _ffn(x, g[0], g[1], w_gate[i, 0], w_up[i, 0], w_down[i, 0])
        axial = i % 2 == 1
        cos_t, sin_t = tables[i % 2]
        if axial:
            qg = q_norm[i // 2].reshape(HEAD_DIM, 1)
            kg = k_norm[i // 2].reshape(HEAD_DIM, 1)
            qt, k, vt = _qkv(x.reshape(b, t, d), g[2], w_qkv[i], cos_t, sin_t, qg, kg,
                             axial=True, vch=TK_DENSE)
            a = _dense_attention(qt, k, vt)
        else:
            qt, k, vt = _qkv(x.reshape(b, t, d), g[2], w_qkv[i], cos_t, sin_t, ones, ones,
                             axial=False, vch=WINDOW)
            a = _window_attention(qt, k, vt, attn_sink[i // 2])
        x = _oproj(a.reshape(b * t, Q_DIM), x, g[3], w_o[i])
        x = _ffn(x, g[4], g[5], w_gate[i, 1], w_up[i, 1], w_down[i, 1])
    return x.reshape(b, t, d)


def kernel(x_prompt, x_sample, norm_g, w_qkv, w_o, attn_sink, q_norm, k_norm, w_gate, w_up, w_down):
    weights = (norm_g, w_qkv.astype(BF16), w_o.astype(BF16), attn_sink, q_norm, k_norm,
               w_gate.astype(BF16), w_up.astype(BF16), w_down.astype(BF16))
    return (_encoder(x_prompt, *weights), _encoder(x_sample, *weights))
```

```python
import functools

import jax
import jax.numpy as jnp
from jax import lax
from jax.experimental import pallas as pl
from jax.experimental.pallas import tpu as pltpu

D_MODEL = 1024
DEPTH = 4
N_HEADS = 16
N_KV_HEADS = 4
HEAD_DIM = D_MODEL // N_HEADS
GROUP = N_HEADS // N_KV_HEADS
Q_DIM = N_HEADS * HEAD_DIM
KV_DIM = N_KV_HEADS * HEAD_DIM
QKV_DIM = Q_DIM + 2 * KV_DIM
D_FF = 2816
WINDOW = 128
GRID_W = 64
ROPE_THETA = 10000.0
EPS = 1e-6

BF16 = jnp.bfloat16
F32 = jnp.float32

MASKED_SCORE = -1e30
LOG2E = 1.4426950408889634
VMEM_LIMIT_BYTES = 56 * 1024 * 1024

K_AUG = 128
KNORM_ROWS = 8
BOUND_SLACK = 1.01
STATIC_BOUND_MAX = 46.0

TM_FFN = 512
TM_QKV = 512
TQ_DENSE = 256
TK_DENSE = 512
DENSE_CHUNKS_PER_TRIP = 8
TQ_WIN = 256


def _rms(x, g):
    return (x * lax.rsqrt(jnp.mean(x * x, axis=-1, keepdims=True) + EPS)) * g


def _resident(shape):
    return pl.BlockSpec(shape, lambda *_: (0,) * len(shape), pipeline_mode=pl.Buffered(1))


def _params(*semantics):
    return pltpu.CompilerParams(dimension_semantics=semantics, vmem_limit_bytes=VMEM_LIMIT_BYTES)


def _ffn_kernel(x_ref, gpre_ref, gpost_ref, wg_ref, wu_ref, wd_ref, o_ref):
    x = x_ref[...]
    h = _rms(x, gpre_ref[...]).astype(BF16)
    gate = jnp.dot(h, wg_ref[...], preferred_element_type=F32)
    up = jnp.dot(h, wu_ref[...], preferred_element_type=F32)
    act = (gate * jax.nn.sigmoid(gate) * up).astype(BF16)
    y = jnp.dot(act, wd_ref[...], preferred_element_type=F32)
    o_ref[...] = x + 0.5 * _rms(y, gpost_ref[...])


def _ffn(x, g_pre, g_post, wg, wu, wd):
    n = x.shape[0]
    tm = TM_FFN
    row = pl.BlockSpec((tm, D_MODEL), lambda i: (i, 0))
    return pl.pallas_call(
        _ffn_kernel,
        grid=(n // tm,),
        in_specs=[row, _resident((1, D_MODEL)), _resident((1, D_MODEL)),
                  _resident((D_MODEL, D_FF)), _resident((D_MODEL, D_FF)), _resident((D_FF, D_MODEL))],
        out_specs=row,
        out_shape=jax.ShapeDtypeStruct((n, D_MODEL), F32),
        compiler_params=_params("parallel"),
        name="ffn",
    )(x, g_pre, g_post, wg, wu, wd)


def _rotate_half(x, axial):
    if axial:
        q = HEAD_DIM // 4
        parts = [x[:, q:2 * q], x[:, 0:q], x[:, 3 * q:4 * q], x[:, 2 * q:3 * q]]
    else:
        h = HEAD_DIM // 2
        parts = [x[:, h:], x[:, :h]]
    return jnp.concatenate(parts, axis=1)


def _head_post(xt, gain, cos, sin, axial):
    if axial:
        ms = jnp.mean(xt * xt, axis=1, keepdims=True)
        xt = (xt * lax.rsqrt(ms + EPS)) * gain[None]
    return xt * cos[None] + _rotate_half(xt, axial) * sin[None]


def _with_ones(xt, rows):
    heads, _, n = xt.shape
    pad_shape = (heads, rows - HEAD_DIM, n)
    pad = (lax.broadcasted_iota(jnp.int32, pad_shape, 1) == 0).astype(xt.dtype)
    return jnp.concatenate([xt, pad], axis=1).reshape(heads * rows, n)


def _qkv_kernel(x_ref, g_ref, w_ref, cos_ref, sin_ref, qg_ref, kg_ref,
                qt_ref, k_ref, vt_ref, kn_ref, *, axial, vch):
    tm = x_ref.shape[1]
    h = _rms(x_ref[0], g_ref[...]).astype(BF16)
    qkv = jnp.dot(h, w_ref[...], preferred_element_type=F32)
    cos = cos_ref[...]
    sin = sin_ref[...]

    qt = qkv[:, :Q_DIM].T.reshape(N_HEADS, HEAD_DIM, tm)
    qt = _head_post(qt, qg_ref[...], cos, sin, axial) * (HEAD_DIM ** -0.5 * LOG2E)
    qt_ref[0] = qt.reshape(Q_DIM, tm).astype(BF16)

    kt = qkv[:, Q_DIM:Q_DIM + KV_DIM].T.reshape(N_KV_HEADS, HEAD_DIM, tm)
    kt = _with_ones(_head_post(kt, kg_ref[...], cos, sin, axial), K_AUG)
    k_ref[0] = kt.T.astype(BF16)
    kf = kt.astype(BF16).astype(F32).reshape(N_KV_HEADS, K_AUG, tm)
    kn = jnp.sqrt(jnp.sum(kf * kf, axis=1))
    kn_ref[0] = jnp.concatenate([kn, jnp.zeros((KNORM_ROWS - N_KV_HEADS, tm), F32)], axis=0)

    v = qkv[:, Q_DIM + KV_DIM:]
    for j in range(tm // vch):
        vt_ref[0, j] = v[j * vch:(j + 1) * vch, :].T.astype(BF16)


def _qkv(x, g, w, cos_t, sin_t, q_gain, k_gain, *, axial, vch):
    b, t, _ = x.shape
    tm = TM_QKV
    k_width = N_KV_HEADS * K_AUG
    v_rows = KV_DIM
    kern = functools.partial(_qkv_kernel, axial=axial, vch=vch)
    return pl.pallas_call(
        kern,
        grid=(b, t // tm),
        in_specs=[pl.BlockSpec((1, tm, D_MODEL), lambda bi, i: (bi, i, 0)),
                  _resident((1, D_MODEL)),
                  _resident((D_MODEL, QKV_DIM)),
                  pl.BlockSpec((HEAD_DIM, tm), lambda bi, i: (0, i)),
                  pl.BlockSpec((HEAD_DIM, tm), lambda bi, i: (0, i)),
                  _resident((HEAD_DIM, 1)), _resident((HEAD_DIM, 1))],
        out_specs=[pl.BlockSpec((1, Q_DIM, tm), lambda bi, i: (bi, 0, i)),
                   pl.BlockSpec((1, tm, k_width), lambda bi, i: (bi, i, 0)),
                   pl.BlockSpec((1, tm // vch, v_rows, vch), lambda bi, i: (bi, i, 0, 0)),
                   pl.BlockSpec((1, KNORM_ROWS, tm), lambda bi, i: (bi, 0, i))],
        out_shape=[jax.ShapeDtypeStruct((b, Q_DIM, t), BF16),
                   jax.ShapeDtypeStruct((b, t, k_width), BF16),
                   jax.ShapeDtypeStruct((b, t // vch, v_rows, vch), BF16),
                   jax.ShapeDtypeStruct((b, KNORM_ROWS, t), F32)],
        compiler_params=_params("parallel", "parallel"),
        name="qkv_axial" if axial else "qkv_window",
    )(x, g, w, cos_t, sin_t, q_gain, k_gain)


def _store_heads(o_ref, ot, kh, tq):
    stacked = jnp.concatenate([ot[:, g * tq:(g + 1) * tq] for g in range(GROUP)], axis=0)
    width = GROUP * HEAD_DIM
    o_ref[0, :, kh * width:(kh + 1) * width] = stacked.T.astype(o_ref.dtype)


def _score_bound(qt_ref, kh, key_norm):
    qnorm = []
    for g in range(GROUP):
        h = kh * GROUP + g
        qf = qt_ref[0, h * HEAD_DIM:(h + 1) * HEAD_DIM, :].astype(F32)
        qnorm.append(jnp.sqrt(jnp.sum(qf * qf, axis=0, keepdims=True)))
    return jnp.concatenate(qnorm, axis=1) * (key_norm * BOUND_SLACK)


def _fill_query_block(qaug, qt_ref, kh, offset, tq):
    for g in range(GROUP):
        h = kh * GROUP + g
        qaug[0:HEAD_DIM, g * tq:(g + 1) * tq] = qt_ref[0, h * HEAD_DIM:(h + 1) * HEAD_DIM, :]
    tail_rows = lax.broadcasted_iota(jnp.int32, (K_AUG - HEAD_DIM, GROUP * tq), 0)
    qaug[HEAD_DIM:, :] = jnp.where(tail_rows == 0, offset, 0.0).astype(BF16)


def _dense_attn_kernel(qt_ref, k_ref, vt_ref, kn_ref, o_ref, qaug_ref, pa_ref, pb_ref, acc_ref, l_ref,
                       *, tq, tk):
    t = k_ref.shape[1]
    n = GROUP * tq
    nc = t // tk

    def rowsum8(p):
        return jnp.sum(p.reshape(tk // 8, 8, n), axis=0)

    for kh in range(N_KV_HEADS):
        kcols = slice(kh * K_AUG, (kh + 1) * K_AUG)
        vrows = slice(kh * HEAD_DIM, (kh + 1) * HEAD_DIM)

        bound = _score_bound(qt_ref, kh, jnp.max(kn_ref[0, kh:kh + 1, :]))
        static_ok = jnp.max(bound) <= STATIC_BOUND_MAX
        _fill_query_block(qaug_ref, qt_ref, kh, jnp.where(static_ok, -bound, 0.0), tq)
        acc_ref[...] = jnp.zeros(acc_ref.shape, F32)
        l_ref[...] = jnp.zeros(l_ref.shape, F32)

        def scores(c, kcols=kcols):
            off = pl.multiple_of(c * tk, tk)
            return jnp.dot(k_ref[0, pl.ds(off, tk), kcols], qaug_ref[...],
                           preferred_element_type=F32)

        def produce(c, dst):
            p = jnp.exp2(scores(c))
            l_ref[...] += rowsum8(p)
            dst[...] = p.astype(BF16)

        def consume(c, src, vrows=vrows):
            acc_ref[...] += jnp.dot(vt_ref[0, c, vrows, :], src[...], preferred_element_type=F32)

        @pl.when(static_ok)
        def _():
            bufs = (pa_ref, pb_ref)
            unroll = DENSE_CHUNKS_PER_TRIP
            produce(0, pa_ref)

            def trip(first, last_trip):
                for j in range(unroll):
                    if not (last_trip and j == unroll - 1):
                        produce(first + j + 1, bufs[(j + 1) % 2])
                    consume(first + j, bufs[j % 2])

            def body(tr, carry):
                trip(tr * unroll, False)
                return carry

            if nc // unroll > 1:
                lax.fori_loop(0, nc // unroll - 1, body, 0)
            trip(nc - unroll, True)

        @pl.when(jnp.logical_not(static_ok))
        def _():
            def chunk(c, m):
                s = scores(c)
                m_new = jnp.maximum(m, jnp.max(s, axis=0, keepdims=True))
                alpha = jnp.exp2(m - m_new)
                p = jnp.exp2(s - m_new)
                l_ref[...] = alpha * l_ref[...] + rowsum8(p)
                pa_ref[...] = p.astype(BF16)
                acc_ref[...] *= alpha
                consume(c, pa_ref)
                return m_new

            lax.fori_loop(0, nc, chunk, jnp.full((1, n), -jnp.inf, F32))

        denom = jnp.sum(l_ref[...], axis=0, keepdims=True)
        _store_heads(o_ref, acc_ref[...] * (1.0 / denom), kh, tq)


def _dense_attention(qt, k, vt, kn):
    b, _, t = qt.shape
    tq, tk = TQ_DENSE, TK_DENSE
    n = GROUP * tq
    kern = functools.partial(_dense_attn_kernel, tq=tq, tk=tk)
    return pl.pallas_call(
        kern,
        grid=(b, t // tq),
        in_specs=[pl.BlockSpec((1, Q_DIM, tq), lambda bi, i: (bi, 0, i)),
                  pl.BlockSpec((1, t, N_KV_HEADS * K_AUG), lambda bi, i: (bi, 0, 0)),
                  pl.BlockSpec((1, t // tk, KV_DIM, tk), lambda bi, i: (bi, 0, 0, 0)),
                  pl.BlockSpec((1, KNORM_ROWS, t), lambda bi, i: (bi, 0, 0))],
        out_specs=pl.BlockSpec((1, tq, Q_DIM), lambda bi, i: (bi, i, 0)),
        out_shape=jax.ShapeDtypeStruct((b, t, Q_DIM), BF16),
        scratch_shapes=[pltpu.VMEM((K_AUG, n), BF16),
                        pltpu.VMEM((tk, n), BF16), pltpu.VMEM((tk, n), BF16),
                        pltpu.VMEM((HEAD_DIM, n), F32),
                        pltpu.VMEM((8, n), F32)],
        compiler_params=_params("parallel", "arbitrary"),
        name="attn_dense",
    )(qt, k, vt, kn)


def _window_attn_kernel(qt_ref, kp_ref, kc_ref, kx_ref, vp_ref, vc_ref, vx_ref, np_ref, nc_ref, nx_ref,
                        bias_ref, sink_ref, o_ref, qaug_ref, *, tq):
    i = pl.program_id(1)
    last = pl.num_programs(1) - 1
    w = WINDOW
    keys = jnp.concatenate([kp_ref[0], kc_ref[0], kx_ref[0]], axis=0)
    bias = jnp.concatenate([
        jnp.where(i > 0, bias_ref[:w], MASKED_SCORE),
        bias_ref[w:w + tq],
        jnp.where(i < last, bias_ref[w + tq:], MASKED_SCORE)], axis=0)

    bounds = []
    for kh in range(N_KV_HEADS):
        rows = slice(kh, kh + 1)
        key_norm = jnp.maximum(jnp.max(nc_ref[0, rows, :]),
                               jnp.maximum(jnp.max(np_ref[0, rows, :]), jnp.max(nx_ref[0, rows, :])))
        bounds.append(_score_bound(qt_ref, kh, key_norm))
    worst = bounds[0]
    for bound in bounds[1:]:
        worst = jnp.maximum(worst, bound)
    static_ok = jnp.max(worst) <= STATIC_BOUND_MAX

    def head(kh, static):
        offset = -bounds[kh] if static else jnp.zeros_like(bounds[kh])
        qaug = qaug_ref.at[kh]
        _fill_query_block(qaug, qt_ref, kh, offset, tq)
        rows = slice(kh * HEAD_DIM, (kh + 1) * HEAD_DIM)
        vals = jnp.concatenate([vp_ref[0, 0, rows, :]]
                               + [vc_ref[0, j, rows, :] for j in range(tq // w)]
                               + [vx_ref[0, 0, rows, :]], axis=1)
        s = jnp.dot(keys[:, kh * K_AUG:(kh + 1) * K_AUG], qaug[...],
                    preferred_element_type=F32) + bias
        sink = sink_ref[kh]
        if static:
            p = jnp.exp2(s)
            sink_term = jnp.exp2(sink + offset.astype(BF16).astype(F32))
        else:
            m = jnp.maximum(jnp.max(s, axis=0, keepdims=True), sink)
            p = jnp.exp2(s - m)
            sink_term = jnp.exp2(sink - m)
        denom = jnp.sum(p, axis=0, keepdims=True) + sink_term
        ot = jnp.dot(vals, p.astype(BF16), preferred_element_type=F32)
        _store_heads(o_ref, ot * (1.0 / denom), kh, tq)

    @pl.when(static_ok)
    def _():
        for kh in range(N_KV_HEADS):
            head(kh, True)

    @pl.when(jnp.logical_not(static_ok))
    def _():
        for kh in range(N_KV_HEADS):
            head(kh, False)


def _window_bias(tq):
    span = tq + 2 * WINDOW
    key_rel = jnp.arange(span)[:, None] - WINDOW
    q_rel = (jnp.arange(GROUP * tq) % tq)[None, :]
    return jnp.where(jnp.abs(key_rel - q_rel) <= WINDOW, 0.0, MASKED_SCORE).astype(F32)


def _window_attention(qt, k, vt, kn, sink):
    b, _, t = qt.shape
    tq, w = TQ_WIN, WINDOW
    r = tq // w
    nblk = t // w
    n = GROUP * tq
    span = tq + 2 * w
    k_width = N_KV_HEADS * K_AUG
    sink_rows = jnp.repeat((sink.astype(F32) * LOG2E).reshape(N_KV_HEADS, 1, GROUP), tq, axis=2)
    prev = lambda i: jnp.maximum(i * r - 1, 0)
    nxt = lambda i: jnp.minimum(i * r + r, nblk - 1)
    kern = functools.partial(_window_attn_kernel, tq=tq)
    return pl.pallas_call(
        kern,
        grid=(b, t // tq),
        in_specs=[pl.BlockSpec((1, Q_DIM, tq), lambda bi, i: (bi, 0, i)),
                  pl.BlockSpec((1, w, k_width), lambda bi, i: (bi, prev(i), 0)),
                  pl.BlockSpec((1, tq, k_width), lambda bi, i: (bi, i, 0)),
                  pl.BlockSpec((1, w, k_width), lambda bi, i: (bi, nxt(i), 0)),
                  pl.BlockSpec((1, 1, KV_DIM, w), lambda bi, i: (bi, prev(i), 0, 0)),
                  pl.BlockSpec((1, r, KV_DIM, w), lambda bi, i: (bi, i, 0, 0)),
                  pl.BlockSpec((1, 1, KV_DIM, w), lambda bi, i: (bi, nxt(i), 0, 0)),
                  pl.BlockSpec((1, KNORM_ROWS, w), lambda bi, i: (bi, 0, prev(i))),
                  pl.BlockSpec((1, KNORM_ROWS, tq), lambda bi, i: (bi, 0, i)),
                  pl.BlockSpec((1, KNORM_ROWS, w), lambda bi, i: (bi, 0, nxt(i))),
                  _resident((span, n)),
                  _resident((N_KV_HEADS, 1, n))],
        out_specs=pl.BlockSpec((1, tq, Q_DIM), lambda bi, i: (bi, i, 0)),
        out_shape=jax.ShapeDtypeStruct((b, t, Q_DIM), BF16),
        scratch_shapes=[pltpu.VMEM((N_KV_HEADS, K_AUG, n), BF16)],
        compiler_params=_params("parallel", "arbitrary"),
        name="attn_window",
    )(qt, k, k, k, vt, vt, vt, kn, kn, kn, _window_bias(tq), sink_rows)


def _oproj_kernel(a_ref, x_ref, g_ref, w_ref, o_ref):
    y = jnp.dot(a_ref[...], w_ref[...], preferred_element_type=F32)
    o_ref[...] = x_ref[...] + _rms(y, g_ref[...])


def _oproj(a, x, g, w):
    n = x.shape[0]
    tm = TM_FFN
    row = pl.BlockSpec((tm, D_MODEL), lambda i: (i, 0))
    return pl.pallas_call(
        _oproj_kernel,
        grid=(n // tm,),
        in_specs=[row, row, _resident((1, D_MODEL)), _resident((Q_DIM, D_MODEL))],
        out_specs=row,
        out_shape=jax.ShapeDtypeStruct((n, D_MODEL), F32),
        compiler_params=_params("parallel"),
        name="oproj",
    )(a, x, g, w)


def _signed_tables(ang):
    cos = jnp.cos(ang)
    sin = jnp.sin(ang)
    return jnp.concatenate([cos, cos], axis=1), jnp.concatenate([-sin, sin], axis=1)


def _window_tables(t):
    half = HEAD_DIM // 2
    inv_freq = ROPE_THETA ** (-jnp.arange(half, dtype=F32) / half)
    pos = jnp.arange(t, dtype=F32)
    cos, sin = _signed_tables(pos[:, None] * inv_freq[None, :])
    return cos.T, sin.T


def _axial_tables(t):
    half = HEAD_DIM // 4
    rows = t // GRID_W
    inv_freq = ROPE_THETA ** (-jnp.arange(half, dtype=F32) / half)
    row = jnp.repeat(jnp.arange(rows, dtype=F32), GRID_W)
    col = jnp.tile(jnp.arange(GRID_W, dtype=F32), rows)
    cos_r, sin_r = _signed_tables(row[:, None] * inv_freq[None, :])
    cos_c, sin_c = _signed_tables(col[:, None] * inv_freq[None, :])
    return (jnp.concatenate([cos_r, cos_c], axis=1).T,
            jnp.concatenate([sin_r, sin_c], axis=1).T)


def _encoder(x, norm_g, w_qkv, w_o, attn_sink, q_norm, k_norm, w_gate, w_up, w_down):
    b, t, d = x.shape
    tables = (_window_tables(t), _axial_tables(t))
    ones = jnp.ones((HEAD_DIM, 1), F32)
    x = x.reshape(b * t, d)
    for i in range(DEPTH):
        g = norm_g[i].reshape(6, 1, D_MODEL)
        x = _ffn(x, g[0], g[1], w_gate[i, 0], w_up[i, 0], w_down[i, 0])
        axial = i % 2 == 1
        cos_t, sin_t = tables[i % 2]
        if axial:
            qg = q_norm[i // 2].reshape(HEAD_DIM, 1)
            kg = k_norm[i // 2].reshape(HEAD_DIM, 1)
            qt, k, vt, kn = _qkv(x.reshape(b, t, d), g[2], w_qkv[i], cos_t, sin_t, qg, kg,
                                 axial=True, vch=TK_DENSE)
            a = _dense_attention(qt, k, vt, kn)
        else:
            qt, k, vt, kn = _qkv(x.reshape(b, t, d), g[2], w_qkv[i], cos_t, sin_t, ones, ones,
                                 axial=False, vch=WINDOW)
            a = _window_attention(qt, k, vt, kn, attn_sink[i // 2])
        x = _oproj(a.reshape(b * t, Q_DIM), x, g[3], w_o[i])
        x = _ffn(x, g[4], g[5], w_gate[i, 1], w_up[i, 1], w_down[i, 1])
    return x.reshape(b, t, d)


def kernel(x_prompt, x_sample, norm_g, w_qkv, w_o, attn_sink, q_norm, k_norm, w_gate, w_up, w_down):
    weights = (norm_g, w_qkv.astype(BF16), w_o.astype(BF16), attn_sink, q_norm, k_norm,
               w_gate.astype(BF16), w_up.astype(BF16), w_down.astype(BF16))
    return (_encoder(x_prompt, *weights), _encoder(x_sample, *weights))
```

```python
import functools

import jax
import jax.numpy as jnp
from jax import lax
from jax.experimental import pallas as pl
from jax.experimental.pallas import tpu as pltpu

D_MODEL = 1024
DEPTH = 4
N_HEADS = 16
N_KV_HEADS = 4
HEAD_DIM = D_MODEL // N_HEADS
GROUP = N_HEADS // N_KV_HEADS
Q_DIM = N_HEADS * HEAD_DIM
KV_DIM = N_KV_HEADS * HEAD_DIM
QKV_DIM = Q_DIM + 2 * KV_DIM
D_FF = 2816
WINDOW = 128
GRID_W = 64
ROPE_THETA = 10000.0
EPS = 1e-6

BF16 = jnp.bfloat16
F32 = jnp.float32

MASKED_SCORE = -1e30
LOG2E = 1.4426950408889634
VMEM_LIMIT_BYTES = 56 * 1024 * 1024

K_AUG = 128
KNORM_ROWS = 8
BOUND_SLACK = 1.01
STATIC_BOUND_MAX = 46.0

TM_FFN = 512
TM_QKV = 512
TQ_DENSE = 256
TK_DENSE = 256
DENSE_CHUNKS_PER_TRIP = 16
TQ_WIN = 256


def _rms(x, g):
    return (x * lax.rsqrt(jnp.mean(x * x, axis=-1, keepdims=True) + EPS)) * g


def _resident(shape):
    return pl.BlockSpec(shape, lambda *_: (0,) * len(shape), pipeline_mode=pl.Buffered(1))


def _params(*semantics):
    return pltpu.CompilerParams(dimension_semantics=semantics, vmem_limit_bytes=VMEM_LIMIT_BYTES)


def _ffn_half_step(x, gpre_ref, gpost_ref, wg_ref, wu_ref, wd_ref):
    h = _rms(x, gpre_ref[...]).astype(BF16)
    gate = jnp.dot(h, wg_ref[...], preferred_element_type=F32)
    up = jnp.dot(h, wu_ref[...], preferred_element_type=F32)
    act = (gate * jax.nn.sigmoid(gate) * up).astype(BF16)
    y = jnp.dot(act, wd_ref[...], preferred_element_type=F32)
    return x + 0.5 * _rms(y, gpost_ref[...])


def _ffn_kernel(x_ref, gpre_ref, gpost_ref, wg_ref, wu_ref, wd_ref, o_ref):
    o_ref[...] = _ffn_half_step(x_ref[...], gpre_ref, gpost_ref, wg_ref, wu_ref, wd_ref)


def _mixer_out_ffn_kernel(a_ref, x_ref, gmix_ref, wo_ref, gpre_ref, gpost_ref, wg_ref, wu_ref, wd_ref,
                          o_ref):
    m = jnp.dot(a_ref[...], wo_ref[...], preferred_element_type=F32)
    x = x_ref[...] + _rms(m, gmix_ref[...])
    o_ref[...] = _ffn_half_step(x, gpre_ref, gpost_ref, wg_ref, wu_ref, wd_ref)


_FFN_WEIGHT_SPECS = ((1, D_MODEL), (1, D_MODEL), (D_MODEL, D_FF), (D_MODEL, D_FF), (D_FF, D_MODEL))


def _ffn(x, g_pre, g_post, wg, wu, wd):
    n = x.shape[0]
    tm = TM_FFN
    row = pl.BlockSpec((tm, D_MODEL), lambda i: (i, 0))
    return pl.pallas_call(
        _ffn_kernel,
        grid=(n // tm,),
        in_specs=[row] + [_resident(s) for s in _FFN_WEIGHT_SPECS],
        out_specs=row,
        out_shape=jax.ShapeDtypeStruct((n, D_MODEL), F32),
        compiler_params=_params("parallel"),
        name="ffn",
    )(x, g_pre, g_post, wg, wu, wd)


def _mixer_out_ffn(a, x, g_mix, wo, g_pre, g_post, wg, wu, wd):
    n = x.shape[0]
    tm = TM_FFN
    row = pl.BlockSpec((tm, D_MODEL), lambda i: (i, 0))
    return pl.pallas_call(
        _mixer_out_ffn_kernel,
        grid=(n // tm,),
        in_specs=[row, row, _resident((1, D_MODEL)), _resident((Q_DIM, D_MODEL))]
                 + [_resident(s) for s in _FFN_WEIGHT_SPECS],
        out_specs=row,
        out_shape=jax.ShapeDtypeStruct((n, D_MODEL), F32),
        compiler_params=_params("parallel"),
        name="mixer_out_ffn",
    )(a, x, g_mix, wo, g_pre, g_post, wg, wu, wd)


def _rotate_half(x, axial):
    if axial:
        q = HEAD_DIM // 4
        parts = [x[:, q:2 * q], x[:, 0:q], x[:, 3 * q:4 * q], x[:, 2 * q:3 * q]]
    else:
        h = HEAD_DIM // 2
        parts = [x[:, h:], x[:, :h]]
    return jnp.concatenate(parts, axis=1)


def _head_post(xt, gain, cos, sin, axial):
    if axial:
        ms = jnp.mean(xt * xt, axis=1, keepdims=True)
        xt = (xt * lax.rsqrt(ms + EPS)) * gain[None]
    return xt * cos[None] + _rotate_half(xt, axial) * sin[None]


def _with_ones(xt, rows):
    heads, _, n = xt.shape
    pad_shape = (heads, rows - HEAD_DIM, n)
    pad = (lax.broadcasted_iota(jnp.int32, pad_shape, 1) == 0).astype(xt.dtype)
    return jnp.concatenate([xt, pad], axis=1).reshape(heads * rows, n)


def _qkv_kernel(x_ref, g_ref, w_ref, cos_ref, sin_ref, qg_ref, kg_ref,
                qt_ref, k_ref, vt_ref, kn_ref, *, axial, vch):
    tm = x_ref.shape[1]
    h = _rms(x_ref[0], g_ref[...]).astype(BF16)
    qkv = jnp.dot(h, w_ref[...], preferred_element_type=F32)
    cos = cos_ref[...]
    sin = sin_ref[...]

    qt = qkv[:, :Q_DIM].T.reshape(N_HEADS, HEAD_DIM, tm)
    qt = _head_post(qt, qg_ref[...], cos, sin, axial) * (HEAD_DIM ** -0.5 * LOG2E)
    qt_ref[0] = qt.reshape(Q_DIM, tm).astype(BF16)

    kt = qkv[:, Q_DIM:Q_DIM + KV_DIM].T.reshape(N_KV_HEADS, HEAD_DIM, tm)
    kt = _with_ones(_head_post(kt, kg_ref[...], cos, sin, axial), K_AUG)
    for kh in range(N_KV_HEADS):
        k_ref[0, kh] = kt[kh * K_AUG:(kh + 1) * K_AUG].T.astype(BF16)
    kf = kt.astype(BF16).astype(F32).reshape(N_KV_HEADS, K_AUG, tm)
    kn = jnp.sqrt(jnp.sum(kf * kf, axis=1))
    kn_ref[0] = jnp.concatenate([kn, jnp.zeros((KNORM_ROWS - N_KV_HEADS, tm), F32)], axis=0)

    v = qkv[:, Q_DIM + KV_DIM:]
    for j in range(tm // vch):
        vt_ref[0, j] = v[j * vch:(j + 1) * vch, :].T.astype(BF16)


def _qkv(x, g, w, cos_t, sin_t, q_gain, k_gain, *, axial, vch):
    b, t, _ = x.shape
    tm = TM_QKV
    v_rows = KV_DIM
    kern = functools.partial(_qkv_kernel, axial=axial, vch=vch)
    return pl.pallas_call(
        kern,
        grid=(b, t // tm),
        in_specs=[pl.BlockSpec((1, tm, D_MODEL), lambda bi, i: (bi, i, 0)),
                  _resident((1, D_MODEL)),
                  _resident((D_MODEL, QKV_DIM)),
                  pl.BlockSpec((HEAD_DIM, tm), lambda bi, i: (0, i)),
                  pl.BlockSpec((HEAD_DIM, tm), lambda bi, i: (0, i)),
                  _resident((HEAD_DIM, 1)), _resident((HEAD_DIM, 1))],
        out_specs=[pl.BlockSpec((1, Q_DIM, tm), lambda bi, i: (bi, 0, i)),
                   pl.BlockSpec((1, N_KV_HEADS, tm, K_AUG), lambda bi, i: (bi, 0, i, 0)),
                   pl.BlockSpec((1, tm // vch, v_rows, vch), lambda bi, i: (bi, i, 0, 0)),
                   pl.BlockSpec((1, KNORM_ROWS, tm), lambda bi, i: (bi, 0, i))],
        out_shape=[jax.ShapeDtypeStruct((b, Q_DIM, t), BF16),
                   jax.ShapeDtypeStruct((b, N_KV_HEADS, t, K_AUG), BF16),
                   jax.ShapeDtypeStruct((b, t // vch, v_rows, vch), BF16),
                   jax.ShapeDtypeStruct((b, KNORM_ROWS, t), F32)],
        compiler_params=_params("parallel", "parallel"),
        name="qkv_axial" if axial else "qkv_window",
    )(x, g, w, cos_t, sin_t, q_gain, k_gain)


def _store_heads(o_ref, ot, kh, tq):
    stacked = jnp.concatenate([ot[:, g * tq:(g + 1) * tq] for g in range(GROUP)], axis=0)
    width = GROUP * HEAD_DIM
    o_ref[0, :, kh * width:(kh + 1) * width] = stacked.T.astype(o_ref.dtype)


def _score_bound(qt_ref, kh, key_norm):
    qnorm = []
    for g in range(GROUP):
        h = kh * GROUP + g
        qf = qt_ref[0, h * HEAD_DIM:(h + 1) * HEAD_DIM, :].astype(F32)
        qnorm.append(jnp.sqrt(jnp.sum(qf * qf, axis=0, keepdims=True)))
    return jnp.concatenate(qnorm, axis=1) * (key_norm * BOUND_SLACK)


def _fill_query_block(qaug, qt_ref, kh, offset, tq):
    for g in range(GROUP):
        h = kh * GROUP + g
        qaug[0:HEAD_DIM, g * tq:(g + 1) * tq] = qt_ref[0, h * HEAD_DIM:(h + 1) * HEAD_DIM, :]
    tail_rows = lax.broadcasted_iota(jnp.int32, (K_AUG - HEAD_DIM, GROUP * tq), 0)
    qaug[HEAD_DIM:, :] = jnp.where(tail_rows == 0, offset, 0.0).astype(BF16)


def _dense_attn_kernel(qt_ref, k_ref, vt_ref, kn_ref, o_ref, qaug_ref, pa_ref, pb_ref, acc_ref, l_ref,
                       *, tq, tk):
    t = k_ref.shape[2]
    n = GROUP * tq
    nc = t // tk

    bounds = [_score_bound(qt_ref, kh, jnp.max(kn_ref[0, kh:kh + 1, :])) for kh in range(N_KV_HEADS)]
    worst = bounds[0]
    for bound in bounds[1:]:
        worst = jnp.maximum(worst, bound)
    static_ok = jnp.max(worst) <= STATIC_BOUND_MAX
    for kh in range(N_KV_HEADS):
        _fill_query_block(qaug_ref.at[kh], qt_ref, kh, jnp.where(static_ok, -bounds[kh], 0.0), tq)
    acc_ref[...] = jnp.zeros(acc_ref.shape, F32)
    l_ref[...] = jnp.zeros(l_ref.shape, F32)

    def rowsum8(p):
        return jnp.sum(p.reshape(tk // 8, 8, n), axis=0)

    def scores(kh, c):
        off = pl.multiple_of(c * tk, tk)
        return jnp.dot(k_ref[0, kh, pl.ds(off, tk), :], qaug_ref[kh],
                       preferred_element_type=F32)

    def consume(kh, c, src):
        rows = pl.ds(pl.multiple_of(kh * HEAD_DIM, HEAD_DIM), HEAD_DIM)
        acc_ref[kh] += jnp.dot(vt_ref[0, c, rows, :], src[...], preferred_element_type=F32)

    @pl.when(static_ok)
    def _():
        bufs = (pa_ref, pb_ref)
        unroll = DENSE_CHUNKS_PER_TRIP
        steps = N_KV_HEADS * nc

        def produce(f, dst):
            kh, c = f // nc, f % nc
            p = jnp.exp2(scores(kh, c))
            l_ref[kh] += rowsum8(p)
            dst[...] = p.astype(BF16)

        def trip(first, last_trip):
            for j in range(unroll):
                f = first + j
                if not (last_trip and j == unroll - 1):
                    produce(f + 1, bufs[(j + 1) % 2])
                consume(f // nc, f % nc, bufs[j % 2])

        def body(tr, carry):
            trip(tr * unroll, False)
            return carry

        produce(0, pa_ref)
        lax.fori_loop(0, steps // unroll - 1, body, 0)
        trip(steps - unroll, True)

    @pl.when(jnp.logical_not(static_ok))
    def _():
        for kh in range(N_KV_HEADS):
            def chunk(c, m, kh=kh):
                s = scores(kh, c)
                m_new = jnp.maximum(m, jnp.max(s, axis=0, keepdims=True))
                alpha = jnp.exp2(m - m_new)
                p = jnp.exp2(s - m_new)
                l_ref[kh] = alpha * l_ref[kh] + rowsum8(p)
                pa_ref[...] = p.astype(BF16)
                acc_ref[kh] *= alpha
                consume(kh, c, pa_ref)
                return m_new

            lax.fori_loop(0, nc, chunk, jnp.full((1, n), -jnp.inf, F32))

    for kh in range(N_KV_HEADS):
        denom = jnp.sum(l_ref[kh], axis=0, keepdims=True)
        _store_heads(o_ref, acc_ref[kh] * (1.0 / denom), kh, tq)


def _dense_attention(qt, k, vt, kn):
    b, _, t = qt.shape
    tq, tk = TQ_DENSE, TK_DENSE
    n = GROUP * tq
    kern = functools.partial(_dense_attn_kernel, tq=tq, tk=tk)
    return pl.pallas_call(
        kern,
        grid=(b, t // tq),
        in_specs=[pl.BlockSpec((1, Q_DIM, tq), lambda bi, i: (bi, 0, i)),
                  pl.BlockSpec((1, N_KV_HEADS, t, K_AUG), lambda bi, i: (bi, 0, 0, 0)),
                  pl.BlockSpec((1, t // tk, KV_DIM, tk), lambda bi, i: (bi, 0, 0, 0)),
                  pl.BlockSpec((1, KNORM_ROWS, t), lambda bi, i: (bi, 0, 0))],
        out_specs=pl.BlockSpec((1, tq, Q_DIM), lambda bi, i: (bi, i, 0)),
        out_shape=jax.ShapeDtypeStruct((b, t, Q_DIM), BF16),
        scratch_shapes=[pltpu.VMEM((N_KV_HEADS, K_AUG, n), BF16),
                        pltpu.VMEM((tk, n), BF16), pltpu.VMEM((tk, n), BF16),
                        pltpu.VMEM((N_KV_HEADS, HEAD_DIM, n), F32),
                        pltpu.VMEM((N_KV_HEADS, 8, n), F32)],
        compiler_params=_params("parallel", "arbitrary"),
        name="attn_dense",
    )(qt, k, vt, kn)


def _window_attn_kernel(qt_ref, kp_ref, kc_ref, kx_ref, vp_ref, vc_ref, vx_ref, np_ref, nc_ref, nx_ref,
                        bias_ref, sink_ref, o_ref, qaug_ref, *, tq):
    i = pl.program_id(1)
    last = pl.num_programs(1) - 1
    w = WINDOW
    bias = jnp.concatenate([
        jnp.where(i > 0, bias_ref[:w], MASKED_SCORE),
        bias_ref[w:w + tq],
        jnp.where(i < last, bias_ref[w + tq:], MASKED_SCORE)], axis=0)

    bounds = []
    for kh in range(N_KV_HEADS):
        rows = slice(kh, kh + 1)
        key_norm = jnp.maximum(jnp.max(nc_ref[0, rows, :]),
                               jnp.maximum(jnp.max(np_ref[0, rows, :]), jnp.max(nx_ref[0, rows, :])))
        bounds.append(_score_bound(qt_ref, kh, key_norm))
    worst = bounds[0]
    for bound in bounds[1:]:
        worst = jnp.maximum(worst, bound)
    static_ok = jnp.max(worst) <= STATIC_BOUND_MAX

    def head(kh, static):
        offset = -bounds[kh] if static else jnp.zeros_like(bounds[kh])
        qaug = qaug_ref.at[kh]
        _fill_query_block(qaug, qt_ref, kh, offset, tq)
        rows = slice(kh * HEAD_DIM, (kh + 1) * HEAD_DIM)
        vals = jnp.concatenate([vp_ref[0, 0, rows, :]]
                               + [vc_ref[0, j, rows, :] for j in range(tq // w)]
                               + [vx_ref[0, 0, rows, :]], axis=1)
        keys = jnp.concatenate([kp_ref[0, kh], kc_ref[0, kh], kx_ref[0, kh]], axis=0)
        s = jnp.dot(keys, qaug[...], preferred_element_type=F32) + bias
        sink = sink_ref[kh]
        if static:
            p = jnp.exp2(s)
            sink_term = jnp.exp2(sink + offset.astype(BF16).astype(F32))
        else:
            m = jnp.maximum(jnp.max(s, axis=0, keepdims=True), sink)
            p = jnp.exp2(s - m)
            sink_term = jnp.exp2(sink - m)
        denom = jnp.sum(p, axis=0, keepdims=True) + sink_term
        ot = jnp.dot(vals, p.astype(BF16), preferred_element_type=F32)
        _store_heads(o_ref, ot * (1.0 / denom), kh, tq)

    @pl.when(static_ok)
    def _():
        for kh in range(N_KV_HEADS):
            head(kh, True)

    @pl.when(jnp.logical_not(static_ok))
    def _():
        for kh in range(N_KV_HEADS):
            head(kh, False)


def _window_bias(tq):
    span = tq + 2 * WINDOW
    key_rel = jnp.arange(span)[:, None] - WINDOW
    q_rel = (jnp.arange(GROUP * tq) % tq)[None, :]
    return jnp.where(jnp.abs(key_rel - q_rel) <= WINDOW, 0.0, MASKED_SCORE).astype(F32)


def _window_attention(qt, k, vt, kn, sink):
    b, _, t = qt.shape
    tq, w = TQ_WIN, WINDOW
    r = tq // w
    nblk = t // w
    n = GROUP * tq
    span = tq + 2 * w
    sink_rows = jnp.repeat((sink.astype(F32) * LOG2E).reshape(N_KV_HEADS, 1, GROUP), tq, axis=2)
    prev = lambda i: jnp.maximum(i * r - 1, 0)
    nxt = lambda i: jnp.minimum(i * r + r, nblk - 1)
    kern = functools.partial(_window_attn_kernel, tq=tq)
    return pl.pallas_call(
        kern,
        grid=(b, t // tq),
        in_specs=[pl.BlockSpec((1, Q_DIM, tq), lambda bi, i: (bi, 0, i)),
                  pl.BlockSpec((1, N_KV_HEADS, w, K_AUG), lambda bi, i: (bi, 0, prev(i), 0)),
                  pl.BlockSpec((1, N_KV_HEADS, tq, K_AUG), lambda bi, i: (bi, 0, i, 0)),
                  pl.BlockSpec((1, N_KV_HEADS, w, K_AUG), lambda bi, i: (bi, 0, nxt(i), 0)),
                  pl.BlockSpec((1, 1, KV_DIM, w), lambda bi, i: (bi, prev(i), 0, 0)),
                  pl.BlockSpec((1, r, KV_DIM, w), lambda bi, i: (bi, i, 0, 0)),
                  pl.BlockSpec((1, 1, KV_DIM, w), lambda bi, i: (bi, nxt(i), 0, 0)),
                  pl.BlockSpec((1, KNORM_ROWS, w), lambda bi, i: (bi, 0, prev(i))),
                  pl.BlockSpec((1, KNORM_ROWS, tq), lambda bi, i: (bi, 0, i)),
                  pl.BlockSpec((1, KNORM_ROWS, w), lambda bi, i: (bi, 0, nxt(i))),
                  _resident((span, n)),
                  _resident((N_KV_HEADS, 1, n))],
        out_specs=pl.BlockSpec((1, tq, Q_DIM), lambda bi, i: (bi, i, 0)),
        out_shape=jax.ShapeDtypeStruct((b, t, Q_DIM), BF16),
        scratch_shapes=[pltpu.VMEM((N_KV_HEADS, K_AUG, n), BF16)],
        compiler_params=_params("parallel", "arbitrary"),
        name="attn_window",
    )(qt, k, k, k, vt, vt, vt, kn, kn, kn, _window_bias(tq), sink_rows)


def _signed_tables(ang):
    cos = jnp.cos(ang)
    sin = jnp.sin(ang)
    return jnp.concatenate([cos, cos], axis=1), jnp.concatenate([-sin, sin], axis=1)


def _window_tables(t):
    half = HEAD_DIM // 2
    inv_freq = ROPE_THETA ** (-jnp.arange(half, dtype=F32) / half)
    pos = jnp.arange(t, dtype=F32)
    cos, sin = _signed_tables(pos[:, None] * inv_freq[None, :])
    return cos.T, sin.T


def _axial_tables(t):
    half = HEAD_DIM // 4
    rows = t // GRID_W
    inv_freq = ROPE_THETA ** (-jnp.arange(half, dtype=F32) / half)
    row = jnp.repeat(jnp.arange(rows, dtype=F32), GRID_W)
    col = jnp.tile(jnp.arange(GRID_W, dtype=F32), rows)
    cos_r, sin_r = _signed_tables(row[:, None] * inv_freq[None, :])
    cos_c, sin_c = _signed_tables(col[:, None] * inv_freq[None, :])
    return (jnp.concatenate([cos_r, cos_c], axis=1).T,
            jnp.concatenate([sin_r, sin_c], axis=1).T)


def _encoder(x, norm_g, w_qkv, w_o, attn_sink, q_norm, k_norm, w_gate, w_up, w_down):
    b, t, d = x.shape
    tables = (_window_tables(t), _axial_tables(t))
    ones = jnp.ones((HEAD_DIM, 1), F32)
    x = x.reshape(b * t, d)
    for i in range(DEPTH):
        g = norm_g[i].reshape(6, 1, D_MODEL)
        x = _ffn(x, g[0], g[1], w_gate[i, 0], w_up[i, 0], w_down[i, 0])
        axial = i % 2 == 1
        cos_t, sin_t = tables[i % 2]
        if axial:
            qg = q_norm[i // 2].reshape(HEAD_DIM, 1)
            kg = k_norm[i // 2].reshape(HEAD_DIM, 1)
            qt, k, vt, kn = _qkv(x.reshape(b, t, d), g[2], w_qkv[i], cos_t, sin_t, qg, kg,
                                 axial=True, vch=TK_DENSE)
            a = _dense_attention(qt, k, vt, kn)
        else:
            qt, k, vt, kn = _qkv(x.reshape(b, t, d), g[2], w_qkv[i], cos_t, sin_t, ones, ones,
                                 axial=False, vch=WINDOW)
            a = _window_attention(qt, k, vt, kn, attn_sink[i // 2])
        x = _mixer_out_ffn(a.reshape(b * t, Q_DIM), x, g[3], w_o[i],
                           g[4], g[5], w_gate[i, 1], w_up[i, 1], w_down[i, 1])
    return x.reshape(b, t, d)


def kernel(x_prompt, x_sample, norm_g, w_qkv, w_o, attn_sink, q_norm, k_norm, w_gate, w_up, w_down):
    weights = (norm_g, w_qkv.astype(BF16), w_o.astype(BF16), attn_sink, q_norm, k_norm,
               w_gate.astype(BF16), w_up.astype(BF16), w_down.astype(BF16))
    return (_encoder(x_prompt, *weights), _encoder(x_sample, *weights))
```

```python
import functools

import jax
import jax.numpy as jnp
from jax import lax
from jax.experimental import pallas as pl
from jax.experimental.pallas import tpu as pltpu

D_MODEL = 1024
DEPTH = 4
N_HEADS = 16
N_KV_HEADS = 4
HEAD_DIM = D_MODEL // N_HEADS
GROUP = N_HEADS // N_KV_HEADS
Q_DIM = N_HEADS * HEAD_DIM
KV_DIM = N_KV_HEADS * HEAD_DIM
QKV_DIM = Q_DIM + 2 * KV_DIM
D_FF = 2816
WINDOW = 128
GRID_W = 64
ROPE_THETA = 10000.0
EPS = 1e-6

BF16 = jnp.bfloat16
F32 = jnp.float32

MASKED_SCORE = -1e30
LOG2E = 1.4426950408889634
VMEM_LIMIT_BYTES = 56 * 1024 * 1024

K_AUG = 128
KNORM_ROWS = 8
BOUND_SLACK = 1.01
STATIC_BOUND_MAX = 46.0

TM_FFN = 512
TM_QKV = 512
TQ_DENSE = 256
TK_DENSE = 256
DENSE_CHUNKS_PER_TRIP = 16
TQ_WIN = 256


def _rms(x, g):
    return (x * lax.rsqrt(jnp.mean(x * x, axis=-1, keepdims=True) + EPS)) * g


def _resident(shape):
    return pl.BlockSpec(shape, lambda *_: (0,) * len(shape), pipeline_mode=pl.Buffered(1))


def _params(*semantics):
    return pltpu.CompilerParams(dimension_semantics=semantics, vmem_limit_bytes=VMEM_LIMIT_BYTES)


def _ffn_half_step(x, gpre_ref, gpost_ref, wg_ref, wu_ref, wd_ref):
    h = _rms(x, gpre_ref[...]).astype(BF16)
    gate = jnp.dot(h, wg_ref[...], preferred_element_type=F32)
    up = jnp.dot(h, wu_ref[...], preferred_element_type=F32)
    act = (gate * jax.nn.sigmoid(gate) * up).astype(BF16)
    y = jnp.dot(act, wd_ref[...], preferred_element_type=F32)
    return x + 0.5 * _rms(y, gpost_ref[...])


def _ffn_kernel(x_ref, gpre_ref, gpost_ref, wg_ref, wu_ref, wd_ref, o_ref):
    o_ref[...] = _ffn_half_step(x_ref[...], gpre_ref, gpost_ref, wg_ref, wu_ref, wd_ref)


def _mixer_out_ffn_kernel(a_ref, x_ref, gmix_ref, wo_ref, gpre_ref, gpost_ref, wg_ref, wu_ref, wd_ref,
                          o_ref):
    m = jnp.dot(a_ref[...], wo_ref[...], preferred_element_type=F32)
    x = x_ref[...] + _rms(m, gmix_ref[...])
    o_ref[...] = _ffn_half_step(x, gpre_ref, gpost_ref, wg_ref, wu_ref, wd_ref)


_FFN_WEIGHT_SPECS = ((1, D_MODEL), (1, D_MODEL), (D_MODEL, D_FF), (D_MODEL, D_FF), (D_FF, D_MODEL))


def _ffn(x, g_pre, g_post, wg, wu, wd):
    n = x.shape[0]
    tm = TM_FFN
    row = pl.BlockSpec((tm, D_MODEL), lambda i: (i, 0))
    return pl.pallas_call(
        _ffn_kernel,
        grid=(n // tm,),
        in_specs=[row] + [_resident(s) for s in _FFN_WEIGHT_SPECS],
        out_specs=row,
        out_shape=jax.ShapeDtypeStruct((n, D_MODEL), F32),
        compiler_params=_params("parallel"),
        name="ffn",
    )(x, g_pre, g_post, wg, wu, wd)


def _mixer_out_ffn(a, x, g_mix, wo, g_pre, g_post, wg, wu, wd):
    n = x.shape[0]
    tm = TM_FFN
    row = pl.BlockSpec((tm, D_MODEL), lambda i: (i, 0))
    return pl.pallas_call(
        _mixer_out_ffn_kernel,
        grid=(n // tm,),
        in_specs=[row, row, _resident((1, D_MODEL)), _resident((Q_DIM, D_MODEL))]
                 + [_resident(s) for s in _FFN_WEIGHT_SPECS],
        out_specs=row,
        out_shape=jax.ShapeDtypeStruct((n, D_MODEL), F32),
        compiler_params=_params("parallel"),
        name="mixer_out_ffn",
    )(a, x, g_mix, wo, g_pre, g_post, wg, wu, wd)


def _rotate_half(x, axial):
    if axial:
        q = HEAD_DIM // 4
        parts = [x[:, q:2 * q], x[:, 0:q], x[:, 3 * q:4 * q], x[:, 2 * q:3 * q]]
    else:
        h = HEAD_DIM // 2
        parts = [x[:, h:], x[:, :h]]
    return jnp.concatenate(parts, axis=1)


def _head_post(xt, gain, cos, sin, axial):
    if axial:
        ms = jnp.mean(xt * xt, axis=1, keepdims=True)
        xt = (xt * lax.rsqrt(ms + EPS)) * gain[None]
    return xt * cos[None] + _rotate_half(xt, axial) * sin[None]


def _with_ones(xt, rows):
    heads, _, n = xt.shape
    pad_shape = (heads, rows - HEAD_DIM, n)
    pad = (lax.broadcasted_iota(jnp.int32, pad_shape, 1) == 0).astype(xt.dtype)
    return jnp.concatenate([xt, pad], axis=1).reshape(heads * rows, n)


def _qkv_kernel(x_ref, g_ref, w_ref, cos_ref, sin_ref, qg_ref, kg_ref,
                qt_ref, k_ref, vt_ref, qn_ref, kn_ref, *, axial, vch):
    tm = x_ref.shape[1]
    h = _rms(x_ref[0], g_ref[...]).astype(BF16)
    qkv = jnp.dot(h, w_ref[...], preferred_element_type=F32)
    cos = cos_ref[...]
    sin = sin_ref[...]

    qt = qkv[:, :Q_DIM].T.reshape(N_HEADS, HEAD_DIM, tm)
    qt = _head_post(qt, qg_ref[...], cos, sin, axial) * (HEAD_DIM ** -0.5 * LOG2E)
    qt_ref[0] = qt.reshape(Q_DIM, tm).astype(BF16)
    qf = qt.astype(BF16).astype(F32)
    qn_ref[0] = jnp.sqrt(jnp.sum(qf * qf, axis=1))

    kt = qkv[:, Q_DIM:Q_DIM + KV_DIM].T.reshape(N_KV_HEADS, HEAD_DIM, tm)
    kt = _with_ones(_head_post(kt, kg_ref[...], cos, sin, axial), K_AUG)
    for kh in range(N_KV_HEADS):
        k_ref[0, kh] = kt[kh * K_AUG:(kh + 1) * K_AUG].T.astype(BF16)
    kf = kt.astype(BF16).astype(F32).reshape(N_KV_HEADS, K_AUG, tm)
    kn = jnp.sqrt(jnp.sum(kf * kf, axis=1))
    kn_ref[0] = jnp.concatenate([kn, jnp.zeros((KNORM_ROWS - N_KV_HEADS, tm), F32)], axis=0)

    v = qkv[:, Q_DIM + KV_DIM:]
    for j in range(tm // vch):
        vt_ref[0, j] = v[j * vch:(j + 1) * vch, :].T.astype(BF16)


def _qkv(x, g, w, cos_t, sin_t, q_gain, k_gain, *, axial, vch):
    b, t, _ = x.shape
    tm = TM_QKV
    v_rows = KV_DIM
    kern = functools.partial(_qkv_kernel, axial=axial, vch=vch)
    return pl.pallas_call(
        kern,
        grid=(b, t // tm),
        in_specs=[pl.BlockSpec((1, tm, D_MODEL), lambda bi, i: (bi, i, 0)),
                  _resident((1, D_MODEL)),
                  _resident((D_MODEL, QKV_DIM)),
                  pl.BlockSpec((HEAD_DIM, tm), lambda bi, i: (0, i)),
                  pl.BlockSpec((HEAD_DIM, tm), lambda bi, i: (0, i)),
                  _resident((HEAD_DIM, 1)), _resident((HEAD_DIM, 1))],
        out_specs=[pl.BlockSpec((1, Q_DIM, tm), lambda bi, i: (bi, 0, i)),
                   pl.BlockSpec((1, N_KV_HEADS, tm, K_AUG), lambda bi, i: (bi, 0, i, 0)),
                   pl.BlockSpec((1, tm // vch, v_rows, vch), lambda bi, i: (bi, i, 0, 0)),
                   pl.BlockSpec((1, N_HEADS, tm), lambda bi, i: (bi, 0, i)),
                   pl.BlockSpec((1, KNORM_ROWS, tm), lambda bi, i: (bi, 0, i))],
        out_shape=[jax.ShapeDtypeStruct((b, Q_DIM, t), BF16),
                   jax.ShapeDtypeStruct((b, N_KV_HEADS, t, K_AUG), BF16),
                   jax.ShapeDtypeStruct((b, t // vch, v_rows, vch), BF16),
                   jax.ShapeDtypeStruct((b, N_HEADS, t), F32),
                   jax.ShapeDtypeStruct((b, KNORM_ROWS, t), F32)],
        compiler_params=_params("parallel", "parallel"),
        name="qkv_axial" if axial else "qkv_window",
    )(x, g, w, cos_t, sin_t, q_gain, k_gain)


def _store_heads(o_ref, ot, kh, tq):
    stacked = jnp.concatenate([ot[:, g * tq:(g + 1) * tq] for g in range(GROUP)], axis=0)
    width = GROUP * HEAD_DIM
    o_ref[0, :, kh * width:(kh + 1) * width] = stacked.T.astype(o_ref.dtype)


def _score_bounds(qn_ref, key_norms):
    bounds = []
    for kh in range(N_KV_HEADS):
        qnorm = jnp.concatenate([qn_ref[0, kh * GROUP + g:kh * GROUP + g + 1, :] for g in range(GROUP)],
                                axis=1)
        bounds.append(qnorm * (key_norms[kh:kh + 1, :] * BOUND_SLACK))
    worst = bounds[0]
    for bound in bounds[1:]:
        worst = jnp.maximum(worst, bound)
    return bounds, jnp.max(worst) <= STATIC_BOUND_MAX


def _fill_query_block(qaug, qt_ref, kh, offset, tq):
    for g in range(GROUP):
        h = kh * GROUP + g
        qaug[0:HEAD_DIM, g * tq:(g + 1) * tq] = qt_ref[0, h * HEAD_DIM:(h + 1) * HEAD_DIM, :]
    tail_rows = lax.broadcasted_iota(jnp.int32, (K_AUG - HEAD_DIM, GROUP * tq), 0)
    qaug[HEAD_DIM:, :] = jnp.where(tail_rows == 0, offset, 0.0).astype(BF16)


def _dense_attn_kernel(qt_ref, k_ref, vt_ref, qn_ref, kn_ref, o_ref, qaug_ref, pa_ref, pb_ref, acc_ref,
                       l_ref, *, tq, tk):
    t = k_ref.shape[2]
    n = GROUP * tq
    nc = t // tk

    bounds, static_ok = _score_bounds(qn_ref, jnp.max(kn_ref[0], axis=1, keepdims=True))
    for kh in range(N_KV_HEADS):
        _fill_query_block(qaug_ref.at[kh], qt_ref, kh, jnp.where(static_ok, -bounds[kh], 0.0), tq)
    acc_ref[...] = jnp.zeros(acc_ref.shape, F32)
    l_ref[...] = jnp.zeros(l_ref.shape, F32)

    def rowsum8(p):
        return jnp.sum(p.reshape(tk // 8, 8, n), axis=0)

    def scores(kh, c):
        off = pl.multiple_of(c * tk, tk)
        return jnp.dot(k_ref[0, kh, pl.ds(off, tk), :], qaug_ref[kh],
                       preferred_element_type=F32)

    def consume(kh, c, src):
        rows = pl.ds(pl.multiple_of(kh * HEAD_DIM, HEAD_DIM), HEAD_DIM)
        acc_ref[kh] += jnp.dot(vt_ref[0, c, rows, :], src[...], preferred_element_type=F32)

    @pl.when(static_ok)
    def _():
        bufs = (pa_ref, pb_ref)
        unroll = DENSE_CHUNKS_PER_TRIP
        steps = N_KV_HEADS * nc

        def produce(f, dst):
            kh, c = f // nc, f % nc
            p = jnp.exp2(scores(kh, c).astype(BF16))
            l_ref[kh] += rowsum8(p.astype(F32))
            dst[...] = p

        def trip(first, last_trip):
            for j in range(unroll):
                f = first + j
                if not (last_trip and j == unroll - 1):
                    produce(f + 1, bufs[(j + 1) % 2])
                consume(f // nc, f % nc, bufs[j % 2])

        def body(tr, carry):
            trip(tr * unroll, False)
            return carry

        produce(0, pa_ref)
        lax.fori_loop(0, steps // unroll - 1, body, 0)
        trip(steps - unroll, True)

    @pl.when(jnp.logical_not(static_ok))
    def _():
        for kh in range(N_KV_HEADS):
            def chunk(c, m, kh=kh):
                s = scores(kh, c)
                m_new = jnp.maximum(m, jnp.max(s, axis=0, keepdims=True))
                alpha = jnp.exp2(m - m_new)
                p = jnp.exp2(s - m_new)
                l_ref[kh] = alpha * l_ref[kh] + rowsum8(p)
                pa_ref[...] = p.astype(BF16)
                acc_ref[kh] *= alpha
                consume(kh, c, pa_ref)
                return m_new

            lax.fori_loop(0, nc, chunk, jnp.full((1, n), -jnp.inf, F32))

    for kh in range(N_KV_HEADS):
        denom = jnp.sum(l_ref[kh], axis=0, keepdims=True)
        _store_heads(o_ref, acc_ref[kh] * (1.0 / denom), kh, tq)


def _dense_attention(qt, k, vt, qn, kn):
    b, _, t = qt.shape
    tq, tk = TQ_DENSE, TK_DENSE
    n = GROUP * tq
    kern = functools.partial(_dense_attn_kernel, tq=tq, tk=tk)
    return pl.pallas_call(
        kern,
        grid=(b, t // tq),
        in_specs=[pl.BlockSpec((1, Q_DIM, tq), lambda bi, i: (bi, 0, i)),
                  pl.BlockSpec((1, N_KV_HEADS, t, K_AUG), lambda bi, i: (bi, 0, 0, 0)),
                  pl.BlockSpec((1, t // tk, KV_DIM, tk), lambda bi, i: (bi, 0, 0, 0)),
                  pl.BlockSpec((1, N_HEADS, tq), lambda bi, i: (bi, 0, i)),
                  pl.BlockSpec((1, KNORM_ROWS, t), lambda bi, i: (bi, 0, 0))],
        out_specs=pl.BlockSpec((1, tq, Q_DIM), lambda bi, i: (bi, i, 0)),
        out_shape=jax.ShapeDtypeStruct((b, t, Q_DIM), BF16),
        scratch_shapes=[pltpu.VMEM((N_KV_HEADS, K_AUG, n), BF16),
                        pltpu.VMEM((tk, n), BF16), pltpu.VMEM((tk, n), BF16),
                        pltpu.VMEM((N_KV_HEADS, HEAD_DIM, n), F32),
                        pltpu.VMEM((N_KV_HEADS, 8, n), F32)],
        compiler_params=_params("parallel", "arbitrary"),
        name="attn_dense",
    )(qt, k, vt, qn, kn)


def _window_attn_kernel(qt_ref, kp_ref, kc_ref, kx_ref, vp_ref, vc_ref, vx_ref, qn_ref, np_ref, nc_ref,
                        nx_ref, bias_ref, sink_ref, o_ref, qaug_ref, *, tq):
    i = pl.program_id(1)
    last = pl.num_programs(1) - 1
    w = WINDOW
    key_norms = jnp.maximum(jnp.max(nc_ref[0], axis=1, keepdims=True),
                            jnp.maximum(jnp.max(np_ref[0], axis=1, keepdims=True),
                                        jnp.max(nx_ref[0], axis=1, keepdims=True)))
    bounds, static_ok = _score_bounds(qn_ref, key_norms)

    def head(kh, static):
        offset = -bounds[kh] if static else jnp.zeros_like(bounds[kh])
        qaug = qaug_ref.at[kh]
        _fill_query_block(qaug, qt_ref, kh, offset, tq)
        rows = slice(kh * HEAD_DIM, (kh + 1) * HEAD_DIM)
        vals = jnp.concatenate([vp_ref[0, 0, rows, :]]
                               + [vc_ref[0, j, rows, :] for j in range(tq // w)]
                               + [vx_ref[0, 0, rows, :]], axis=1)
        keys = jnp.concatenate([kp_ref[0, kh], kc_ref[0, kh], kx_ref[0, kh]], axis=0)
        bias = jnp.concatenate([
            jnp.where(i > 0, bias_ref[:w], MASKED_SCORE),
            bias_ref[w:w + tq],
            jnp.where(i < last, bias_ref[w + tq:], MASKED_SCORE)], axis=0)
        s = jnp.dot(keys, qaug[...], preferred_element_type=F32) + bias
        sink = sink_ref[kh]
        if static:
            p = jnp.exp2(s.astype(BF16))
            denom = jnp.sum(p.astype(F32), axis=0, keepdims=True)
            sink_term = jnp.exp2(sink + offset.astype(BF16).astype(F32))
        else:
            m = jnp.maximum(jnp.max(s, axis=0, keepdims=True), sink)
            pf = jnp.exp2(s - m)
            denom = jnp.sum(pf, axis=0, keepdims=True)
            p = pf.astype(BF16)
            sink_term = jnp.exp2(sink - m)
        ot = jnp.dot(vals, p, preferred_element_type=F32)
        _store_heads(o_ref, ot * (1.0 / (denom + sink_term)), kh, tq)

    @pl.when(static_ok)
    def _():
        for kh in range(N_KV_HEADS):
            head(kh, True)

    @pl.when(jnp.logical_not(static_ok))
    def _():
        for kh in range(N_KV_HEADS):
            head(kh, False)


def _window_bias(tq):
    span = tq + 2 * WINDOW
    key_rel = jnp.arange(span)[:, None] - WINDOW
    q_rel = (jnp.arange(GROUP * tq) % tq)[None, :]
    return jnp.where(jnp.abs(key_rel - q_rel) <= WINDOW, 0.0, MASKED_SCORE).astype(F32)


def _window_attention(qt, k, vt, qn, kn, sink):
    b, _, t = qt.shape
    tq, w = TQ_WIN, WINDOW
    r = tq // w
    nblk = t // w
    n = GROUP * tq
    span = tq + 2 * w
    sink_rows = jnp.repeat((sink.astype(F32) * LOG2E).reshape(N_KV_HEADS, 1, GROUP), tq, axis=2)
    prev = lambda i: jnp.maximum(i * r - 1, 0)
    nxt = lambda i: jnp.minimum(i * r + r, nblk - 1)
    kern = functools.partial(_window_attn_kernel, tq=tq)
    return pl.pallas_call(
        kern,
        grid=(b, t // tq),
        in_specs=[pl.BlockSpec((1, Q_DIM, tq), lambda bi, i: (bi, 0, i)),
                  pl.BlockSpec((1, N_KV_HEADS, w, K_AUG), lambda bi, i: (bi, 0, prev(i), 0)),
                  pl.BlockSpec((1, N_KV_HEADS, tq, K_AUG), lambda bi, i: (bi, 0, i, 0)),
                  pl.BlockSpec((1, N_KV_HEADS, w, K_AUG), lambda bi, i: (bi, 0, nxt(i), 0)),
                  pl.BlockSpec((1, 1, KV_DIM, w), lambda bi, i: (bi, prev(i), 0, 0)),
                  pl.BlockSpec((1, r, KV_DIM, w), lambda bi, i: (bi, i, 0, 0)),
                  pl.BlockSpec((1, 1, KV_DIM, w), lambda bi, i: (bi, nxt(i), 0, 0)),
                  pl.BlockSpec((1, N_HEADS, tq), lambda bi, i: (bi, 0, i)),
                  pl.BlockSpec((1, KNORM_ROWS, w), lambda bi, i: (bi, 0, prev(i))),
                  pl.BlockSpec((1, KNORM_ROWS, tq), lambda bi, i: (bi, 0, i)),
                  pl.BlockSpec((1, KNORM_ROWS, w), lambda bi, i: (bi, 0, nxt(i))),
                  _resident((span, n)),
                  _resident((N_KV_HEADS, 1, n))],
        out_specs=pl.BlockSpec((1, tq, Q_DIM), lambda bi, i: (bi, i, 0)),
        out_shape=jax.ShapeDtypeStruct((b, t, Q_DIM), BF16),
        scratch_shapes=[pltpu.VMEM((N_KV_HEADS, K_AUG, n), BF16)],
        compiler_params=_params("parallel", "arbitrary"),
        name="attn_window",
    )(qt, k, k, k, vt, vt, vt, qn, kn, kn, kn, _window_bias(tq), sink_rows)


def _signed_tables(ang):
    cos = jnp.cos(ang)
    sin = jnp.sin(ang)
    return jnp.concatenate([cos, cos], axis=1), jnp.concatenate([-sin, sin], axis=1)


def _window_tables(t):
    half = HEAD_DIM // 2
    inv_freq = ROPE_THETA ** (-jnp.arange(half, dtype=F32) / half)
    pos = jnp.arange(t, dtype=F32)
    cos, sin = _signed_tables(pos[:, None] * inv_freq[None, :])
    return cos.T, sin.T


def _axial_tables(t):
    half = HEAD_DIM // 4
    rows = t // GRID_W
    inv_freq = ROPE_THETA ** (-jnp.arange(half, dtype=F32) / half)
    row = jnp.repeat(jnp.arange(rows, dtype=F32), GRID_W)
    col = jnp.tile(jnp.arange(GRID_W, dtype=F32), rows)
    cos_r, sin_r = _signed_tables(row[:, None] * inv_freq[None, :])
    cos_c, sin_c = _signed_tables(col[:, None] * inv_freq[None, :])
    return (jnp.concatenate([cos_r, cos_c], axis=1).T,
            jnp.concatenate([sin_r, sin_c], axis=1).T)


def _encoder(x, norm_g, w_qkv, w_o, attn_sink, q_norm, k_norm, w_gate, w_up, w_down):
    b, t, d = x.shape
    tables = (_window_tables(t), _axial_tables(t))
    ones = jnp.ones((HEAD_DIM, 1), F32)
    x = x.reshape(b * t, d)
    for i in range(DEPTH):
        g = norm_g[i].reshape(6, 1, D_MODEL)
        x = _ffn(x, g[0], g[1], w_gate[i, 0], w_up[i, 0], w_down[i, 0])
        axial = i % 2 == 1
        cos_t, sin_t = tables[i % 2]
        if axial:
            qg = q_norm[i // 2].reshape(HEAD_DIM, 1)
            kg = k_norm[i // 2].reshape(HEAD_DIM, 1)
            qt, k, vt, qn, kn = _qkv(x.reshape(b, t, d), g[2], w_qkv[i], cos_t, sin_t, qg, kg,
                                     axial=True, vch=TK_DENSE)
            a = _dense_attention(qt, k, vt, qn, kn)
        else:
            qt, k, vt, qn, kn = _qkv(x.reshape(b, t, d), g[2], w_qkv[i], cos_t, sin_t, ones, ones,
                                     axial=False, vch=WINDOW)
            a = _window_attention(qt, k, vt, qn, kn, attn_sink[i // 2])
        x = _mixer_out_ffn(a.reshape(b * t, Q_DIM), x, g[3], w_o[i],
                           g[4], g[5], w_gate[i, 1], w_up[i, 1], w_down[i, 1])
    return x.reshape(b, t, d)


def kernel(x_prompt, x_sample, norm_g, w_qkv, w_o, attn_sink, q_norm, k_norm, w_gate, w_up, w_down):
    weights = (norm_g, w_qkv.astype(BF16), w_o.astype(BF16), attn_sink, q_norm, k_norm,
               w_gate.astype(BF16), w_up.astype(BF16), w_down.astype(BF16))
    return (_encoder(x_prompt, *weights), _encoder(x_sample, *weights))
```

```python
import functools

import jax
import jax.numpy as jnp
from jax import lax
from jax.experimental import pallas as pl
from jax.experimental.pallas import tpu as pltpu

D_MODEL = 1024
DEPTH = 4
N_HEADS = 16
N_KV_HEADS = 4
HEAD_DIM = D_MODEL // N_HEADS
GROUP = N_HEADS // N_KV_HEADS
Q_DIM = N_HEADS * HEAD_DIM
KV_DIM = N_KV_HEADS * HEAD_DIM
QKV_DIM = Q_DIM + 2 * KV_DIM
D_FF = 2816
WINDOW = 128
GRID_W = 64
ROPE_THETA = 10000.0
EPS = 1e-6

BF16 = jnp.bfloat16
F32 = jnp.float32

MASKED_SCORE = -1e30
LOG2E = 1.4426950408889634
VMEM_LIMIT_BYTES = 56 * 1024 * 1024

K_AUG = 128
KNORM_ROWS = 8
BOUND_SLACK = 1.01
STATIC_BOUND_MAX = 46.0

TM_FFN = 512
TM_QKV = 512
TQ_DENSE = 256
TK_DENSE = 256
DENSE_CHUNKS_PER_TRIP = 32
TQ_WIN = 256


def _rms(x, g):
    return (x * lax.rsqrt(jnp.mean(x * x, axis=-1, keepdims=True) + EPS)) * g


def _resident(shape):
    return pl.BlockSpec(shape, lambda *_: (0,) * len(shape), pipeline_mode=pl.Buffered(1))


def _params(*semantics):
    return pltpu.CompilerParams(dimension_semantics=semantics, vmem_limit_bytes=VMEM_LIMIT_BYTES)


def _ffn_half_step(x, gpre_ref, gpost_ref, wg_ref, wu_ref, wd_ref):
    h = _rms(x, gpre_ref[...]).astype(BF16)
    gate = jnp.dot(h, wg_ref[...], preferred_element_type=F32)
    up = jnp.dot(h, wu_ref[...], preferred_element_type=F32)
    act = (gate * jax.nn.sigmoid(gate) * up).astype(BF16)
    y = jnp.dot(act, wd_ref[...], preferred_element_type=F32)
    return x + 0.5 * _rms(y, gpost_ref[...])


def _ffn_kernel(x_ref, gpre_ref, gpost_ref, wg_ref, wu_ref, wd_ref, o_ref):
    o_ref[...] = _ffn_half_step(x_ref[...], gpre_ref, gpost_ref, wg_ref, wu_ref, wd_ref)


def _mixer_out_ffn_kernel(a_ref, x_ref, gmix_ref, wo_ref, gpre_ref, gpost_ref, wg_ref, wu_ref, wd_ref,
                          o_ref):
    m = jnp.dot(a_ref[...], wo_ref[...], preferred_element_type=F32)
    x = x_ref[...] + _rms(m, gmix_ref[...])
    o_ref[...] = _ffn_half_step(x, gpre_ref, gpost_ref, wg_ref, wu_ref, wd_ref)


_FFN_WEIGHT_SPECS = ((1, D_MODEL), (1, D_MODEL), (D_MODEL, D_FF), (D_MODEL, D_FF), (D_FF, D_MODEL))


def _ffn(x, g_pre, g_post, wg, wu, wd):
    n = x.shape[0]
    tm = TM_FFN
    row = pl.BlockSpec((tm, D_MODEL), lambda i: (i, 0))
    return pl.pallas_call(
        _ffn_kernel,
        grid=(n // tm,),
        in_specs=[row] + [_resident(s) for s in _FFN_WEIGHT_SPECS],
        out_specs=row,
        out_shape=jax.ShapeDtypeStruct((n, D_MODEL), F32),
        compiler_params=_params("parallel"),
        name="ffn",
    )(x, g_pre, g_post, wg, wu, wd)


def _mixer_out_ffn(a, x, g_mix, wo, g_pre, g_post, wg, wu, wd):
    n = x.shape[0]
    tm = TM_FFN
    row = pl.BlockSpec((tm, D_MODEL), lambda i: (i, 0))
    return pl.pallas_call(
        _mixer_out_ffn_kernel,
        grid=(n // tm,),
        in_specs=[row, row, _resident((1, D_MODEL)), _resident((Q_DIM, D_MODEL))]
                 + [_resident(s) for s in _FFN_WEIGHT_SPECS],
        out_specs=row,
        out_shape=jax.ShapeDtypeStruct((n, D_MODEL), F32),
        compiler_params=_params("parallel"),
        name="mixer_out_ffn",
    )(a, x, g_mix, wo, g_pre, g_post, wg, wu, wd)


def _rotate_half(x, axial):
    if axial:
        q = HEAD_DIM // 4
        parts = [x[:, q:2 * q], x[:, 0:q], x[:, 3 * q:4 * q], x[:, 2 * q:3 * q]]
    else:
        h = HEAD_DIM // 2
        parts = [x[:, h:], x[:, :h]]
    return jnp.concatenate(parts, axis=1)


def _head_post(xt, gain, cos, sin, axial):
    if axial:
        ms = jnp.mean(xt * xt, axis=1, keepdims=True)
        xt = (xt * lax.rsqrt(ms + EPS)) * gain[None]
    return xt * cos[None] + _rotate_half(xt, axial) * sin[None]


def _with_ones(xt, rows):
    heads, _, n = xt.shape
    pad_shape = (heads, rows - HEAD_DIM, n)
    pad = (lax.broadcasted_iota(jnp.int32, pad_shape, 1) == 0).astype(xt.dtype)
    return jnp.concatenate([xt, pad], axis=1).reshape(heads * rows, n)


def _qkv_kernel(x_ref, g_ref, w_ref, cos_ref, sin_ref, qg_ref, kg_ref,
                qt_ref, k_ref, vt_ref, qn_ref, kn_ref, *, axial, vch):
    tm = x_ref.shape[1]
    h = _rms(x_ref[0], g_ref[...]).astype(BF16)
    qkv = jnp.dot(h, w_ref[...], preferred_element_type=F32)
    cos = cos_ref[...]
    sin = sin_ref[...]

    qt = qkv[:, :Q_DIM].T.reshape(N_HEADS, HEAD_DIM, tm)
    qt = _head_post(qt, qg_ref[...], cos, sin, axial) * (HEAD_DIM ** -0.5 * LOG2E)
    qt_ref[0] = qt.reshape(Q_DIM, tm).astype(BF16)
    qn_ref[0] = jnp.sqrt(jnp.sum(qt * qt, axis=1))

    kt = qkv[:, Q_DIM:Q_DIM + KV_DIM].T.reshape(N_KV_HEADS, HEAD_DIM, tm)
    kt = _with_ones(_head_post(kt, kg_ref[...], cos, sin, axial), K_AUG)
    for kh in range(N_KV_HEADS):
        k_ref[0, kh] = kt[kh * K_AUG:(kh + 1) * K_AUG].T.astype(BF16)
    kf = kt.reshape(N_KV_HEADS, K_AUG, tm)
    kn = jnp.sqrt(jnp.sum(kf * kf, axis=1))
    kn_ref[0] = jnp.concatenate([kn, jnp.zeros((KNORM_ROWS - N_KV_HEADS, tm), F32)], axis=0)

    v = qkv[:, Q_DIM + KV_DIM:]
    for j in range(tm // vch):
        vt_ref[0, j] = v[j * vch:(j + 1) * vch, :].T.astype(BF16)


def _qkv(x, g, w, cos_t, sin_t, q_gain, k_gain, *, axial, vch):
    b, t, _ = x.shape
    tm = TM_QKV
    v_rows = KV_DIM
    kern = functools.partial(_qkv_kernel, axial=axial, vch=vch)
    return pl.pallas_call(
        kern,
        grid=(b, t // tm),
        in_specs=[pl.BlockSpec((1, tm, D_MODEL), lambda bi, i: (bi, i, 0)),
                  _resident((1, D_MODEL)),
                  _resident((D_MODEL, QKV_DIM)),
                  pl.BlockSpec((HEAD_DIM, tm), lambda bi, i: (0, i)),
                  pl.BlockSpec((HEAD_DIM, tm), lambda bi, i: (0, i)),
                  _resident((HEAD_DIM, 1)), _resident((HEAD_DIM, 1))],
        out_specs=[pl.BlockSpec((1, Q_DIM, tm), lambda bi, i: (bi, 0, i)),
                   pl.BlockSpec((1, N_KV_HEADS, tm, K_AUG), lambda bi, i: (bi, 0, i, 0)),
                   pl.BlockSpec((1, tm // vch, v_rows, vch), lambda bi, i: (bi, i, 0, 0)),
                   pl.BlockSpec((1, N_HEADS, tm), lambda bi, i: (bi, 0, i)),
                   pl.BlockSpec((1, KNORM_ROWS, tm), lambda bi, i: (bi, 0, i))],
        out_shape=[jax.ShapeDtypeStruct((b, Q_DIM, t), BF16),
                   jax.ShapeDtypeStruct((b, N_KV_HEADS, t, K_AUG), BF16),
                   jax.ShapeDtypeStruct((b, t // vch, v_rows, vch), BF16),
                   jax.ShapeDtypeStruct((b, N_HEADS, t), F32),
                   jax.ShapeDtypeStruct((b, KNORM_ROWS, t), F32)],
        compiler_params=_params("parallel", "parallel"),
        name="qkv_axial" if axial else "qkv_window",
    )(x, g, w, cos_t, sin_t, q_gain, k_gain)


def _store_heads(o_ref, ot, kh, tq):
    stacked = jnp.concatenate([ot[:, g * tq:(g + 1) * tq] for g in range(GROUP)], axis=0)
    width = GROUP * HEAD_DIM
    o_ref[0, :, kh * width:(kh + 1) * width] = stacked.T.astype(o_ref.dtype)


def _score_bounds(qn_ref, key_norms):
    bounds = []
    for kh in range(N_KV_HEADS):
        qnorm = jnp.concatenate([qn_ref[0, kh * GROUP + g:kh * GROUP + g + 1, :] for g in range(GROUP)],
                                axis=1)
        bounds.append(qnorm * (key_norms[kh:kh + 1, :] * BOUND_SLACK))
    worst = bounds[0]
    for bound in bounds[1:]:
        worst = jnp.maximum(worst, bound)
    return bounds, jnp.max(worst) <= STATIC_BOUND_MAX


def _fill_query_block(qaug, qt_ref, kh, offset, tq):
    for g in range(GROUP):
        h = kh * GROUP + g
        qaug[0:HEAD_DIM, g * tq:(g + 1) * tq] = qt_ref[0, h * HEAD_DIM:(h + 1) * HEAD_DIM, :]
    tail_rows = lax.broadcasted_iota(jnp.int32, (K_AUG - HEAD_DIM, GROUP * tq), 0)
    qaug[HEAD_DIM:, :] = jnp.where(tail_rows == 0, offset, 0.0).astype(BF16)


def _dense_attn_kernel(qt_ref, k_ref, vt_ref, qn_ref, kn_ref, o_ref, qaug_ref, pa_ref, pb_ref, acc_ref,
                       l_ref, *, tq, tk):
    t = k_ref.shape[2]
    n = GROUP * tq
    nc = t // tk

    bounds, static_ok = _score_bounds(qn_ref, jnp.max(kn_ref[0], axis=1, keepdims=True))
    for kh in range(N_KV_HEADS):
        _fill_query_block(qaug_ref.at[kh], qt_ref, kh, jnp.where(static_ok, -bounds[kh], 0.0), tq)
    acc_ref[...] = jnp.zeros(acc_ref.shape, F32)
    l_ref[...] = jnp.zeros(l_ref.shape, F32)

    def rowsum8(p):
        return jnp.sum(p.reshape(tk // 8, 8, n), axis=0)

    def scores(kh, c):
        off = pl.multiple_of(c * tk, tk)
        return jnp.dot(k_ref[0, kh, pl.ds(off, tk), :], qaug_ref[kh],
                       preferred_element_type=F32)

    def consume(kh, c, src):
        rows = pl.ds(pl.multiple_of(kh * HEAD_DIM, HEAD_DIM), HEAD_DIM)
        acc_ref[kh] += jnp.dot(vt_ref[0, c, rows, :], src[...], preferred_element_type=F32)

    @pl.when(static_ok)
    def _():
        bufs = (pa_ref, pb_ref)
        unroll = DENSE_CHUNKS_PER_TRIP
        steps = N_KV_HEADS * nc

        def produce(f, dst):
            kh, c = f // nc, f % nc
            p = jnp.exp2(scores(kh, c))
            l_ref[kh] += rowsum8(p)
            dst[...] = p.astype(BF16)

        def trip(first, last_trip):
            for j in range(unroll):
                f = first + j
                if not (last_trip and j == unroll - 1):
                    produce(f + 1, bufs[(j + 1) % 2])
                consume(f // nc, f % nc, bufs[j % 2])

        def body(tr, carry):
            trip(tr * unroll, False)
            return carry

        produce(0, pa_ref)
        lax.fori_loop(0, steps // unroll - 1, body, 0)
        trip(steps - unroll, True)

    @pl.when(jnp.logical_not(static_ok))
    def _():
        for kh in range(N_KV_HEADS):
            def chunk(c, m, kh=kh):
                s = scores(kh, c)
                m_new = jnp.maximum(m, jnp.max(s, axis=0, keepdims=True))
                alpha = jnp.exp2(m - m_new)
                p = jnp.exp2(s - m_new)
                l_ref[kh] = alpha * l_ref[kh] + rowsum8(p)
                pa_ref[...] = p.astype(BF16)
                acc_ref[kh] *= alpha
                consume(kh, c, pa_ref)
                return m_new

            lax.fori_loop(0, nc, chunk, jnp.full((1, n), -jnp.inf, F32))

    for kh in range(N_KV_HEADS):
        denom = jnp.sum(l_ref[kh], axis=0, keepdims=True)
        _store_heads(o_ref, acc_ref[kh] * (1.0 / denom), kh, tq)


def _dense_attention(qt, k, vt, qn, kn):
    b, _, t = qt.shape
    tq, tk = TQ_DENSE, TK_DENSE
    n = GROUP * tq
    kern = functools.partial(_dense_attn_kernel, tq=tq, tk=tk)
    return pl.pallas_call(
        kern,
        grid=(b, t // tq),
        in_specs=[pl.BlockSpec((1, Q_DIM, tq), lambda bi, i: (bi, 0, i)),
                  pl.BlockSpec((1, N_KV_HEADS, t, K_AUG), lambda bi, i: (bi, 0, 0, 0)),
                  pl.BlockSpec((1, t // tk, KV_DIM, tk), lambda bi, i: (bi, 0, 0, 0)),
                  pl.BlockSpec((1, N_HEADS, tq), lambda bi, i: (bi, 0, i)),
                  pl.BlockSpec((1, KNORM_ROWS, t), lambda bi, i: (bi, 0, 0))],
        out_specs=pl.BlockSpec((1, tq, Q_DIM), lambda bi, i: (bi, i, 0)),
        out_shape=jax.ShapeDtypeStruct((b, t, Q_DIM), BF16),
        scratch_shapes=[pltpu.VMEM((N_KV_HEADS, K_AUG, n), BF16),
                        pltpu.VMEM((tk, n), BF16), pltpu.VMEM((tk, n), BF16),
                        pltpu.VMEM((N_KV_HEADS, HEAD_DIM, n), F32),
                        pltpu.VMEM((N_KV_HEADS, 8, n), F32)],
        compiler_params=_params("parallel", "arbitrary"),
        name="attn_dense",
    )(qt, k, vt, qn, kn)


def _window_attn_kernel(qt_ref, kp_ref, kc_ref, kx_ref, vp_ref, vc_ref, vx_ref, qn_ref, np_ref, nc_ref,
                        nx_ref, bias_ref, sink_ref, o_ref, qaug_ref, p_ref, *, tq):
    i = pl.program_id(1)
    last = pl.num_programs(1) - 1
    w = WINDOW
    key_norms = jnp.maximum(jnp.max(nc_ref[0], axis=1, keepdims=True),
                            jnp.maximum(jnp.max(np_ref[0], axis=1, keepdims=True),
                                        jnp.max(nx_ref[0], axis=1, keepdims=True)))
    bounds, static_ok = _score_bounds(qn_ref, key_norms)

    def produce(kh, static):
        offset = -bounds[kh] if static else jnp.zeros_like(bounds[kh])
        qaug = qaug_ref.at[kh]
        _fill_query_block(qaug, qt_ref, kh, offset, tq)
        keys = jnp.concatenate([kp_ref[0, kh], kc_ref[0, kh], kx_ref[0, kh]], axis=0)
        bias = jnp.concatenate([
            jnp.where(i > 0, bias_ref[:w], MASKED_SCORE),
            bias_ref[w:w + tq],
            jnp.where(i < last, bias_ref[w + tq:], MASKED_SCORE)], axis=0)
        s = jnp.dot(keys, qaug[...], preferred_element_type=F32) + bias
        sink = sink_ref[kh]
        if static:
            p = jnp.exp2(s)
            sink_term = jnp.exp2(sink + offset.astype(BF16).astype(F32))
        else:
            m = jnp.maximum(jnp.max(s, axis=0, keepdims=True), sink)
            p = jnp.exp2(s - m)
            sink_term = jnp.exp2(sink - m)
        p_ref[kh % 2] = p.astype(BF16)
        return jnp.sum(p, axis=0, keepdims=True) + sink_term

    def consume(kh, denom):
        rows = slice(kh * HEAD_DIM, (kh + 1) * HEAD_DIM)
        vals = jnp.concatenate([vp_ref[0, 0, rows, :]]
                               + [vc_ref[0, j, rows, :] for j in range(tq // w)]
                               + [vx_ref[0, 0, rows, :]], axis=1)
        ot = jnp.dot(vals, p_ref[kh % 2], preferred_element_type=F32)
        _store_heads(o_ref, ot * (1.0 / denom), kh, tq)

    def heads(static):
        denom = produce(0, static)
        for kh in range(N_KV_HEADS):
            next_denom = produce(kh + 1, static) if kh + 1 < N_KV_HEADS else None
            consume(kh, denom)
            denom = next_denom

    @pl.when(static_ok)
    def _():
        heads(True)

    @pl.when(jnp.logical_not(static_ok))
    def _():
        heads(False)


def _window_bias(tq):
    span = tq + 2 * WINDOW
    key_rel = jnp.arange(span)[:, None] - WINDOW
    q_rel = (jnp.arange(GROUP * tq) % tq)[None, :]
    return jnp.where(jnp.abs(key_rel - q_rel) <= WINDOW, 0.0, MASKED_SCORE).astype(F32)


def _window_attention(qt, k, vt, qn, kn, sink):
    b, _, t = qt.shape
    tq, w = TQ_WIN, WINDOW
    r = tq // w
    nblk = t // w
    n = GROUP * tq
    span = tq + 2 * w
    sink_rows = jnp.repeat((sink.astype(F32) * LOG2E).reshape(N_KV_HEADS, 1, GROUP), tq, axis=2)
    prev = lambda i: jnp.maximum(i * r - 1, 0)
    nxt = lambda i: jnp.minimum(i * r + r, nblk - 1)
    kern = functools.partial(_window_attn_kernel, tq=tq)
    return pl.pallas_call(
        kern,
        grid=(b, t // tq),
        in_specs=[pl.BlockSpec((1, Q_DIM, tq), lambda bi, i: (bi, 0, i)),
                  pl.BlockSpec((1, N_KV_HEADS, w, K_AUG), lambda bi, i: (bi, 0, prev(i), 0)),
                  pl.BlockSpec((1, N_KV_HEADS, tq, K_AUG), lambda bi, i: (bi, 0, i, 0)),
                  pl.BlockSpec((1, N_KV_HEADS, w, K_AUG), lambda bi, i: (bi, 0, nxt(i), 0)),
                  pl.BlockSpec((1, 1, KV_DIM, w), lambda bi, i: (bi, prev(i), 0, 0)),
                  pl.BlockSpec((1, r, KV_DIM, w), lambda bi, i: (bi, i, 0, 0)),
                  pl.BlockSpec((1, 1, KV_DIM, w), lambda bi, i: (bi, nxt(i), 0, 0)),
                  pl.BlockSpec((1, N_HEADS, tq), lambda bi, i: (bi, 0, i)),
                  pl.BlockSpec((1, KNORM_ROWS, w), lambda bi, i: (bi, 0, prev(i))),
                  pl.BlockSpec((1, KNORM_ROWS, tq), lambda bi, i: (bi, 0, i)),
                  pl.BlockSpec((1, KNORM_ROWS, w), lambda bi, i: (bi, 0, nxt(i))),
                  _resident((span, n)),
                  _resident((N_KV_HEADS, 1, n))],
        out_specs=pl.BlockSpec((1, tq, Q_DIM), lambda bi, i: (bi, i, 0)),
        out_shape=jax.ShapeDtypeStruct((b, t, Q_DIM), BF16),
        scratch_shapes=[pltpu.VMEM((N_KV_HEADS, K_AUG, n), BF16),
                        pltpu.VMEM((2, span, n), BF16)],
        compiler_params=_params("parallel", "arbitrary"),
        name="attn_window",
    )(qt, k, k, k, vt, vt, vt, qn, kn, kn, kn, _window_bias(tq), sink_rows)


def _signed_tables(ang):
    cos = jnp.cos(ang)
    sin = jnp.sin(ang)
    return jnp.concatenate([cos, cos], axis=1), jnp.concatenate([-sin, sin], axis=1)


def _window_tables(t):
    half = HEAD_DIM // 2
    inv_freq = ROPE_THETA ** (-jnp.arange(half, dtype=F32) / half)
    pos = jnp.arange(t, dtype=F32)
    cos, sin = _signed_tables(pos[:, None] * inv_freq[None, :])
    return cos.T, sin.T


def _axial_tables(t):
    half = HEAD_DIM // 4
    rows = t // GRID_W
    inv_freq = ROPE_THETA ** (-jnp.arange(half, dtype=F32) / half)
    row = jnp.repeat(jnp.arange(rows, dtype=F32), GRID_W)
    col = jnp.tile(jnp.arange(GRID_W, dtype=F32), rows)
    cos_r, sin_r = _signed_tables(row[:, None] * inv_freq[None, :])
    cos_c, sin_c = _signed_tables(col[:, None] * inv_freq[None, :])
    return (jnp.concatenate([cos_r, cos_c], axis=1).T,
            jnp.concatenate([sin_r, sin_c], axis=1).T)


def _encoder(x, norm_g, w_qkv, w_o, attn_sink, q_norm, k_norm, w_gate, w_up, w_down):
    b, t, d = x.shape
    tables = (_window_tables(t), _axial_tables(t))
    ones = jnp.ones((HEAD_DIM, 1), F32)
    x = x.reshape(b * t, d)
    for i in range(DEPTH):
        g = norm_g[i].reshape(6, 1, D_MODEL)
        x = _ffn(x, g[0], g[1], w_gate[i, 0], w_up[i, 0], w_down[i, 0])
        axial = i % 2 == 1
        cos_t, sin_t = tables[i % 2]
        if axial:
            qg = q_norm[i // 2].reshape(HEAD_DIM, 1)
            kg = k_norm[i // 2].reshape(HEAD_DIM, 1)
            qt, k, vt, qn, kn = _qkv(x.reshape(b, t, d), g[2], w_qkv[i], cos_t, sin_t, qg, kg,
                                     axial=True, vch=TK_DENSE)
            a = _dense_attention(qt, k, vt, qn, kn)
        else:
            qt, k, vt, qn, kn = _qkv(x.reshape(b, t, d), g[2], w_qkv[i], cos_t, sin_t, ones, ones,
                                     axial=False, vch=WINDOW)
            a = _window_attention(qt, k, vt, qn, kn, attn_sink[i // 2])
        x = _mixer_out_ffn(a.reshape(b * t, Q_DIM), x, g[3], w_o[i],
                           g[4], g[5], w_gate[i, 1], w_up[i, 1], w_down[i, 1])
    return x.reshape(b, t, d)


def kernel(x_prompt, x_sample, norm_g, w_qkv, w_o, attn_sink, q_norm, k_norm, w_gate, w_up, w_down):
    weights = (norm_g, w_qkv.astype(BF16), w_o.astype(BF16), attn_sink, q_norm, k_norm,
               w_gate.astype(BF16), w_up.astype(BF16), w_down.astype(BF16))
    return (_encoder(x_prompt, *weights), _encoder(x_sample, *weights))
```

```python
import functools

import jax
import jax.numpy as jnp
from jax import lax
from jax.experimental import pallas as pl
from jax.experimental.pallas import tpu as pltpu

D_MODEL = 1024
DEPTH = 4
N_HEADS = 16
N_KV_HEADS = 4
HEAD_DIM = D_MODEL // N_HEADS
GROUP = N_HEADS // N_KV_HEADS
Q_DIM = N_HEADS * HEAD_DIM
KV_DIM = N_KV_HEADS * HEAD_DIM
QKV_DIM = Q_DIM + 2 * KV_DIM
D_FF = 2816
WINDOW = 128
GRID_W = 64
ROPE_THETA = 10000.0
EPS = 1e-6
NORM_FFN_PRE, NORM_FFN_POST = (0, 4), (1, 5)
NORM_MIXER_PRE, NORM_MIXER_POST = 2, 3

BF16 = jnp.bfloat16
F32 = jnp.float32

MASKED_SCORE = -1e30
LOG2E = 1.4426950408889634
VMEM_LIMIT_BYTES = 56 * 1024 * 1024

SUBLANES = 8
K_AUG = 128
KNORM_ROWS = SUBLANES
BOUND_SLACK = 1.01
STATIC_BOUND_MAX = 46.0

TM_FFN = 1024
TM_QKV = 1024
TQ_DENSE = 256
TK_DENSE = 256
DENSE_CHUNKS_PER_TRIP = 32
TQ_WIN = 256


def _rms(x, g):
    return (x * lax.rsqrt(jnp.mean(x * x, axis=-1, keepdims=True) + EPS)) * g


def _resident(shape):
    return pl.BlockSpec(shape, lambda *_: (0,) * len(shape), pipeline_mode=pl.Buffered(1))


def _resident_slice(arr, *lead):
    rest = arr.shape[len(lead):]
    return pl.BlockSpec((None,) * len(lead) + rest, lambda *_: lead + (0,) * len(rest),
                        pipeline_mode=pl.Buffered(1))


def _params(*semantics):
    return pltpu.CompilerParams(dimension_semantics=semantics, vmem_limit_bytes=VMEM_LIMIT_BYTES)


def _ffn_half_step(x, gpre_ref, gpost_ref, wg_ref, wu_ref, wd_ref):
    h = _rms(x, gpre_ref[...]).astype(BF16)
    gate = jnp.dot(h, wg_ref[...], preferred_element_type=F32)
    up = jnp.dot(h, wu_ref[...], preferred_element_type=F32)
    act = (gate * jax.nn.sigmoid(gate) * up).astype(BF16)
    y = jnp.dot(act, wd_ref[...], preferred_element_type=F32)
    return x + 0.5 * _rms(y, gpost_ref[...])


def _ffn_kernel(x_ref, gpre_ref, gpost_ref, wg_ref, wu_ref, wd_ref, o_ref):
    o_ref[...] = _ffn_half_step(x_ref[...], gpre_ref, gpost_ref, wg_ref, wu_ref, wd_ref)


def _mixer_out_ffn_kernel(a_ref, x_ref, gmix_ref, wo_ref, gpre_ref, gpost_ref, wg_ref, wu_ref, wd_ref,
                          o_ref):
    m = jnp.dot(a_ref[...], wo_ref[...], preferred_element_type=F32)
    x = x_ref[...] + _rms(m, gmix_ref[...])
    o_ref[...] = _ffn_half_step(x, gpre_ref, gpost_ref, wg_ref, wu_ref, wd_ref)


def _ffn_operands(p, layer, half):
    arrays = (p["norm_g"], p["norm_g"], p["w_gate"], p["w_up"], p["w_down"])
    specs = [_resident_slice(p["norm_g"], layer, NORM_FFN_PRE[half]),
             _resident_slice(p["norm_g"], layer, NORM_FFN_POST[half])]
    specs += [_resident_slice(p[w], layer, half) for w in ("w_gate", "w_up", "w_down")]
    return arrays, specs


def _ffn(x, p, layer):
    n = x.shape[0]
    tm = TM_FFN
    row = pl.BlockSpec((tm, D_MODEL), lambda i: (i, 0))
    arrays, specs = _ffn_operands(p, layer, 0)
    return pl.pallas_call(
        _ffn_kernel,
        grid=(n // tm,),
        in_specs=[row] + specs,
        out_specs=row,
        out_shape=jax.ShapeDtypeStruct((n, D_MODEL), F32),
        compiler_params=_params("parallel"),
        name="ffn",
    )(x, *arrays)


def _mixer_out_ffn(a, x, p, layer):
    n = x.shape[0]
    tm = TM_FFN
    row = pl.BlockSpec((tm, D_MODEL), lambda i: (i, 0))
    arrays, specs = _ffn_operands(p, layer, 1)
    return pl.pallas_call(
        _mixer_out_ffn_kernel,
        grid=(n // tm,),
        in_specs=[row, row, _resident_slice(p["norm_g"], layer, NORM_MIXER_POST),
                  _resident_slice(p["w_o"], layer)] + specs,
        out_specs=row,
        out_shape=jax.ShapeDtypeStruct((n, D_MODEL), F32),
        compiler_params=_params("parallel"),
        name="mixer_out_ffn",
    )(a, x, p["norm_g"], p["w_o"], *arrays)


def _rotate_half(x, axial):
    if axial:
        q = HEAD_DIM // 4
        parts = [x[:, q:2 * q], x[:, 0:q], x[:, 3 * q:4 * q], x[:, 2 * q:3 * q]]
    else:
        h = HEAD_DIM // 2
        parts = [x[:, h:], x[:, :h]]
    return jnp.concatenate(parts, axis=1)


def _head_post(xt, gain_ref, cos, sin, axial):
    if axial:
        ms = jnp.mean(xt * xt, axis=1, keepdims=True)
        xt = (xt * lax.rsqrt(ms + EPS)) * gain_ref[...][None]
    return xt * cos[None] + _rotate_half(xt, axial) * sin[None]


def _with_ones(xt, rows):
    heads, _, n = xt.shape
    pad_shape = (heads, rows - HEAD_DIM, n)
    pad = (lax.broadcasted_iota(jnp.int32, pad_shape, 1) == 0).astype(xt.dtype)
    return jnp.concatenate([xt, pad], axis=1).reshape(heads * rows, n)


def _qkv_kernel(x_ref, g_ref, w_ref, cos_ref, sin_ref, *refs, axial, vch):
    qg_ref, kg_ref = refs[:-5] if axial else (None, None)
    qt_ref, k_ref, vt_ref, qn_ref, kn_ref = refs[-5:]
    tm = x_ref.shape[1]
    h = _rms(x_ref[0], g_ref[...]).astype(BF16)
    qkv = jnp.dot(h, w_ref[...], preferred_element_type=F32)
    cos = cos_ref[...]
    sin = sin_ref[...]

    qt = qkv[:, :Q_DIM].T.reshape(N_HEADS, HEAD_DIM, tm)
    qt = _head_post(qt, qg_ref, cos, sin, axial) * (HEAD_DIM ** -0.5 * LOG2E)
    qt_ref[0] = qt.reshape(Q_DIM, tm).astype(BF16)
    qn_ref[0] = jnp.sqrt(jnp.sum(qt * qt, axis=1))

    kt = qkv[:, Q_DIM:Q_DIM + KV_DIM].T.reshape(N_KV_HEADS, HEAD_DIM, tm)
    kt = _with_ones(_head_post(kt, kg_ref, cos, sin, axial), K_AUG)
    for kh in range(N_KV_HEADS):
        k_ref[0, kh] = kt[kh * K_AUG:(kh + 1) * K_AUG].T.astype(BF16)
    kf = kt.reshape(N_KV_HEADS, K_AUG, tm)
    kn = jnp.sqrt(jnp.sum(kf * kf, axis=1))
    kn_ref[0] = jnp.concatenate([kn, jnp.zeros((KNORM_ROWS - N_KV_HEADS, tm), F32)], axis=0)

    v = qkv[:, Q_DIM + KV_DIM:]
    for j in range(tm // vch):
        vt_ref[0, j] = v[j * vch:(j + 1) * vch, :].T.astype(BF16)


def _qkv(x, p, layer, cos_t, sin_t, *, axial, vch):
    b, t, _ = x.shape
    tm = TM_QKV
    v_rows = KV_DIM
    kern = functools.partial(_qkv_kernel, axial=axial, vch=vch)
    gains = (p["q_norm"], p["k_norm"]) if axial else ()
    return pl.pallas_call(
        kern,
        grid=(b, t // tm),
        in_specs=[pl.BlockSpec((1, tm, D_MODEL), lambda bi, i: (bi, i, 0)),
                  _resident_slice(p["norm_g"], layer, NORM_MIXER_PRE),
                  _resident_slice(p["w_qkv"], layer),
                  pl.BlockSpec((HEAD_DIM, tm), lambda bi, i: (0, i)),
                  pl.BlockSpec((HEAD_DIM, tm), lambda bi, i: (0, i))]
                 + [_resident_slice(gain, layer // 2) for gain in gains],
        out_specs=[pl.BlockSpec((1, Q_DIM, tm), lambda bi, i: (bi, 0, i)),
                   pl.BlockSpec((1, N_KV_HEADS, tm, K_AUG), lambda bi, i: (bi, 0, i, 0)),
                   pl.BlockSpec((1, tm // vch, v_rows, vch), lambda bi, i: (bi, i, 0, 0)),
                   pl.BlockSpec((1, N_HEADS, tm), lambda bi, i: (bi, 0, i)),
                   pl.BlockSpec((1, KNORM_ROWS, tm), lambda bi, i: (bi, 0, i))],
        out_shape=[jax.ShapeDtypeStruct((b, Q_DIM, t), BF16),
                   jax.ShapeDtypeStruct((b, N_KV_HEADS, t, K_AUG), BF16),
                   jax.ShapeDtypeStruct((b, t // vch, v_rows, vch), BF16),
                   jax.ShapeDtypeStruct((b, N_HEADS, t), F32),
                   jax.ShapeDtypeStruct((b, KNORM_ROWS, t), F32)],
        compiler_params=_params("parallel", "parallel"),
        name="qkv_axial" if axial else "qkv_window",
    )(x, p["norm_g"], p["w_qkv"], cos_t, sin_t, *gains)


def _store_heads(o_ref, ot, kh, tq):
    stacked = jnp.concatenate([ot[:, g * tq:(g + 1) * tq] for g in range(GROUP)], axis=0)
    width = GROUP * HEAD_DIM
    o_ref[0, :, kh * width:(kh + 1) * width] = stacked.T.astype(o_ref.dtype)


def _score_bounds(qn_ref, key_norms):
    bounds = []
    for kh in range(N_KV_HEADS):
        qnorm = jnp.concatenate([qn_ref[0, kh * GROUP + g:kh * GROUP + g + 1, :] for g in range(GROUP)],
                                axis=1)
        bounds.append(qnorm * (key_norms[kh:kh + 1, :] * BOUND_SLACK))
    worst = bounds[0]
    for bound in bounds[1:]:
        worst = jnp.maximum(worst, bound)
    return bounds, jnp.max(worst) <= STATIC_BOUND_MAX


def _fill_query_block(qaug, qt_ref, kh, offset, tq):
    for g in range(GROUP):
        h = kh * GROUP + g
        qaug[0:HEAD_DIM, g * tq:(g + 1) * tq] = qt_ref[0, h * HEAD_DIM:(h + 1) * HEAD_DIM, :]
    tail_rows = lax.broadcasted_iota(jnp.int32, (K_AUG - HEAD_DIM, GROUP * tq), 0)
    qaug[HEAD_DIM:, :] = jnp.where(tail_rows == 0, offset, 0.0).astype(BF16)


def _dense_attn_kernel(qt_ref, k_ref, vt_ref, qn_ref, kn_ref, o_ref, qaug_ref, pa_ref, pb_ref, acc_ref,
                       l_ref, *, tq, tk):
    t = k_ref.shape[2]
    n = GROUP * tq
    nc = t // tk

    bounds, static_ok = _score_bounds(qn_ref, jnp.max(kn_ref[0], axis=1, keepdims=True))
    for kh in range(N_KV_HEADS):
        _fill_query_block(qaug_ref.at[kh], qt_ref, kh, jnp.where(static_ok, -bounds[kh], 0.0), tq)
    acc_ref[...] = jnp.zeros(acc_ref.shape, F32)
    l_ref[...] = jnp.zeros(l_ref.shape, F32)

    def rowsum8(p):
        return jnp.sum(p.reshape(tk // SUBLANES, SUBLANES, n), axis=0)

    def scores(kh, c):
        off = pl.multiple_of(c * tk, tk)
        return jnp.dot(k_ref[0, kh, pl.ds(off, tk), :], qaug_ref[kh],
                       preferred_element_type=F32)

    def consume(kh, c, src):
        rows = pl.ds(pl.multiple_of(kh * HEAD_DIM, HEAD_DIM), HEAD_DIM)
        acc_ref[kh] += jnp.dot(vt_ref[0, c, rows, :], src[...], preferred_element_type=F32)

    @pl.when(static_ok)
    def _():
        bufs = (pa_ref, pb_ref)
        unroll = DENSE_CHUNKS_PER_TRIP
        steps = N_KV_HEADS * nc

        def produce(f, dst):
            kh, c = f // nc, f % nc
            p = jnp.exp2(scores(kh, c))
            l_ref[kh] += rowsum8(p)
            dst[...] = p.astype(BF16)

        def trip(first, last_trip):
            for j in range(unroll):
                f = first + j
                if not (last_trip and j == unroll - 1):
                    produce(f + 1, bufs[(j + 1) % 2])
                consume(f // nc, f % nc, bufs[j % 2])

        def body(tr, carry):
            trip(tr * unroll, False)
            return carry

        produce(0, pa_ref)
        lax.fori_loop(0, steps // unroll - 1, body, 0)
        trip(steps - unroll, True)

    @pl.when(jnp.logical_not(static_ok))
    def _():
        for kh in range(N_KV_HEADS):
            def chunk(c, m, kh=kh):
                s = scores(kh, c)
                m_new = jnp.maximum(m, jnp.max(s, axis=0, keepdims=True))
                alpha = jnp.exp2(m - m_new)
                p = jnp.exp2(s - m_new)
                l_ref[kh] = alpha * l_ref[kh] + rowsum8(p)
                pa_ref[...] = p.astype(BF16)
                acc_ref[kh] *= alpha
                consume(kh, c, pa_ref)
                return m_new

            lax.fori_loop(0, nc, chunk, jnp.full((1, n), -jnp.inf, F32))

    for kh in range(N_KV_HEADS):
        denom = jnp.sum(l_ref[kh], axis=0, keepdims=True)
        _store_heads(o_ref, acc_ref[kh] * (1.0 / denom), kh, tq)


def _dense_attention(qt, k, vt, qn, kn):
    b, _, t = qt.shape
    tq, tk = TQ_DENSE, TK_DENSE
    n = GROUP * tq
    kern = functools.partial(_dense_attn_kernel, tq=tq, tk=tk)
    return pl.pallas_call(
        kern,
        grid=(b, t // tq),
        in_specs=[pl.BlockSpec((1, Q_DIM, tq), lambda bi, i: (bi, 0, i)),
                  pl.BlockSpec((1, N_KV_HEADS, t, K_AUG), lambda bi, i: (bi, 0, 0, 0)),
                  pl.BlockSpec((1, t // tk, KV_DIM, tk), lambda bi, i: (bi, 0, 0, 0)),
                  pl.BlockSpec((1, N_HEADS, tq), lambda bi, i: (bi, 0, i)),
                  pl.BlockSpec((1, KNORM_ROWS, t), lambda bi, i: (bi, 0, 0))],
        out_specs=pl.BlockSpec((1, tq, Q_DIM), lambda bi, i: (bi, i, 0)),
        out_shape=jax.ShapeDtypeStruct((b, t, Q_DIM), BF16),
        scratch_shapes=[pltpu.VMEM((N_KV_HEADS, K_AUG, n), BF16),
                        pltpu.VMEM((tk, n), BF16), pltpu.VMEM((tk, n), BF16),
                        pltpu.VMEM((N_KV_HEADS, HEAD_DIM, n), F32),
                        pltpu.VMEM((N_KV_HEADS, SUBLANES, n), F32)],
        compiler_params=_params("parallel", "arbitrary"),
        name="attn_dense",
    )(qt, k, vt, qn, kn)


def _window_attn_kernel(qt_ref, kp_ref, kc_ref, kx_ref, vp_ref, vc_ref, vx_ref, qn_ref, np_ref, nc_ref,
                        nx_ref, bias_ref, sink_ref, o_ref, qaug_ref, p_ref, *, tq):
    i = pl.program_id(1)
    last = pl.num_programs(1) - 1
    w = WINDOW
    key_norms = jnp.maximum(jnp.max(nc_ref[0], axis=1, keepdims=True),
                            jnp.maximum(jnp.max(np_ref[0], axis=1, keepdims=True),
                                        jnp.max(nx_ref[0], axis=1, keepdims=True)))
    bounds, static_ok = _score_bounds(qn_ref, key_norms)

    def produce(kh, static):
        offset = -bounds[kh] if static else jnp.zeros_like(bounds[kh])
        qaug = qaug_ref.at[kh]
        _fill_query_block(qaug, qt_ref, kh, offset, tq)
        keys = jnp.concatenate([kp_ref[0, kh], kc_ref[0, kh], kx_ref[0, kh]], axis=0)
        bias = jnp.concatenate([
            jnp.where(i > 0, bias_ref[:w], MASKED_SCORE),
            bias_ref[w:w + tq],
            jnp.where(i < last, bias_ref[w + tq:], MASKED_SCORE)], axis=0)
        s = jnp.dot(keys, qaug[...], preferred_element_type=F32) + bias
        sink = sink_ref[kh]
        if static:
            p = jnp.exp2(s)
            sink_term = jnp.exp2(sink + offset.astype(BF16).astype(F32))
        else:
            m = jnp.maximum(jnp.max(s, axis=0, keepdims=True), sink)
            p = jnp.exp2(s - m)
            sink_term = jnp.exp2(sink - m)
        p_ref[kh % 2] = p.astype(BF16)
        return jnp.sum(p, axis=0, keepdims=True) + sink_term

    def consume(kh, denom):
        rows = slice(kh * HEAD_DIM, (kh + 1) * HEAD_DIM)
        vals = jnp.concatenate([vp_ref[0, 0, rows, :]]
                               + [vc_ref[0, j, rows, :] for j in range(tq // w)]
                               + [vx_ref[0, 0, rows, :]], axis=1)
        ot = jnp.dot(vals, p_ref[kh % 2], preferred_element_type=F32)
        _store_heads(o_ref, ot * (1.0 / denom), kh, tq)

    def heads(static):
        denom = produce(0, static)
        for kh in range(N_KV_HEADS):
            next_denom = produce(kh + 1, static) if kh + 1 < N_KV_HEADS else None
            consume(kh, denom)
            denom = next_denom

    @pl.when(static_ok)
    def _():
        heads(True)

    @pl.when(jnp.logical_not(static_ok))
    def _():
        heads(False)


def _window_bias(tq):
    span = tq + 2 * WINDOW
    key_rel = jnp.arange(span)[:, None] - WINDOW
    q_rel = (jnp.arange(GROUP * tq) % tq)[None, :]
    return jnp.where(jnp.abs(key_rel - q_rel) <= WINDOW, 0.0, MASKED_SCORE).astype(F32)


def _window_attention(qt, k, vt, qn, kn, sink):
    b, _, t = qt.shape
    tq, w = TQ_WIN, WINDOW
    r = tq // w
    nblk = t // w
    n = GROUP * tq
    span = tq + 2 * w
    sink_rows = jnp.repeat((sink.astype(F32) * LOG2E).reshape(N_KV_HEADS, 1, GROUP), tq, axis=2)
    prev = lambda i: jnp.maximum(i * r - 1, 0)
    nxt = lambda i: jnp.minimum(i * r + r, nblk - 1)
    kern = functools.partial(_window_attn_kernel, tq=tq)
    return pl.pallas_call(
        kern,
        grid=(b, t // tq),
        in_specs=[pl.BlockSpec((1, Q_DIM, tq), lambda bi, i: (bi, 0, i)),
                  pl.BlockSpec((1, N_KV_HEADS, w, K_AUG), lambda bi, i: (bi, 0, prev(i), 0)),
                  pl.BlockSpec((1, N_KV_HEADS, tq, K_AUG), lambda bi, i: (bi, 0, i, 0)),
                  pl.BlockSpec((1, N_KV_HEADS, w, K_AUG), lambda bi, i: (bi, 0, nxt(i), 0)),
                  pl.BlockSpec((1, 1, KV_DIM, w), lambda bi, i: (bi, prev(i), 0, 0)),
                  pl.BlockSpec((1, r, KV_DIM, w), lambda bi, i: (bi, i, 0, 0)),
                  pl.BlockSpec((1, 1, KV_DIM, w), lambda bi, i: (bi, nxt(i), 0, 0)),
                  pl.BlockSpec((1, N_HEADS, tq), lambda bi, i: (bi, 0, i)),
                  pl.BlockSpec((1, KNORM_ROWS, w), lambda bi, i: (bi, 0, prev(i))),
                  pl.BlockSpec((1, KNORM_ROWS, tq), lambda bi, i: (bi, 0, i)),
                  pl.BlockSpec((1, KNORM_ROWS, w), lambda bi, i: (bi, 0, nxt(i))),
                  _resident((span, n)),
                  _resident((N_KV_HEADS, 1, n))],
        out_specs=pl.BlockSpec((1, tq, Q_DIM), lambda bi, i: (bi, i, 0)),
        out_shape=jax.ShapeDtypeStruct((b, t, Q_DIM), BF16),
        scratch_shapes=[pltpu.VMEM((N_KV_HEADS, K_AUG, n), BF16),
                        pltpu.VMEM((2, span, n), BF16)],
        compiler_params=_params("parallel", "arbitrary"),
        name="attn_window",
    )(qt, k, k, k, vt, vt, vt, qn, kn, kn, kn, _window_bias(tq), sink_rows)


def _signed_tables(ang):
    cos = jnp.cos(ang)
    sin = jnp.sin(ang)
    return jnp.concatenate([cos, cos], axis=1), jnp.concatenate([-sin, sin], axis=1)


def _window_tables(t):
    half = HEAD_DIM // 2
    inv_freq = ROPE_THETA ** (-jnp.arange(half, dtype=F32) / half)
    pos = jnp.arange(t, dtype=F32)
    cos, sin = _signed_tables(pos[:, None] * inv_freq[None, :])
    return cos.T, sin.T


def _axial_tables(t):
    half = HEAD_DIM // 4
    rows = t // GRID_W
    inv_freq = ROPE_THETA ** (-jnp.arange(half, dtype=F32) / half)
    row = jnp.repeat(jnp.arange(rows, dtype=F32), GRID_W)
    col = jnp.tile(jnp.arange(GRID_W, dtype=F32), rows)
    cos_r, sin_r = _signed_tables(row[:, None] * inv_freq[None, :])
    cos_c, sin_c = _signed_tables(col[:, None] * inv_freq[None, :])
    return (jnp.concatenate([cos_r, cos_c], axis=1).T,
            jnp.concatenate([sin_r, sin_c], axis=1).T)


def _encoder(x, p):
    b, t, d = x.shape
    tables = (_window_tables(t), _axial_tables(t))
    x = x.reshape(b * t, d)
    for layer in range(DEPTH):
        x = _ffn(x, p, layer)
        axial = layer % 2 == 1
        cos_t, sin_t = tables[layer % 2]
        if axial:
            qt, k, vt, qn, kn = _qkv(x.reshape(b, t, d), p, layer, cos_t, sin_t, axial=True, vch=TK_DENSE)
            a = _dense_attention(qt, k, vt, qn, kn)
        else:
            qt, k, vt, qn, kn = _qkv(x.reshape(b, t, d), p, layer, cos_t, sin_t, axial=False, vch=WINDOW)
            a = _window_attention(qt, k, vt, qn, kn, p["attn_sink"][layer // 2])
        x = _mixer_out_ffn(a.reshape(b * t, Q_DIM), x, p, layer)
    return x.reshape(b, t, d)


def kernel(x_prompt, x_sample, norm_g, w_qkv, w_o, attn_sink, q_norm, k_norm, w_gate, w_up, w_down):
    p = {"norm_g": norm_g.reshape(DEPTH, -1, 1, D_MODEL),
         "q_norm": q_norm.reshape(-1, HEAD_DIM, 1), "k_norm": k_norm.reshape(-1, HEAD_DIM, 1),
         "attn_sink": attn_sink,
         "w_qkv": w_qkv.astype(BF16), "w_o": w_o.astype(BF16),
         "w_gate": w_gate.astype(BF16), "w_up": w_up.astype(BF16), "w_down": w_down.astype(BF16)}
    return (_encoder(x_prompt, p), _encoder(x_sample, p))
```

```python
import functools

import jax
import jax.numpy as jnp
from jax import lax
from jax.experimental import pallas as pl
from jax.experimental.pallas import tpu as pltpu

D_MODEL = 1024
DEPTH = 4
N_HEADS = 16
N_KV_HEADS = 4
HEAD_DIM = D_MODEL // N_HEADS
GROUP = N_HEADS // N_KV_HEADS
Q_DIM = N_HEADS * HEAD_DIM
KV_DIM = N_KV_HEADS * HEAD_DIM
QKV_DIM = Q_DIM + 2 * KV_DIM
D_FF = 2816
WINDOW = 128
GRID_W = 64
ROPE_THETA = 10000.0
EPS = 1e-6
NORM_FFN_PRE, NORM_FFN_POST = (0, 4), (1, 5)
NORM_MIXER_PRE, NORM_MIXER_POST = 2, 3

BF16 = jnp.bfloat16
F32 = jnp.float32

MASKED_SCORE = -1e30
LOG2E = 1.4426950408889634
VMEM_LIMIT_BYTES = 56 * 1024 * 1024

SUBLANES = 8
K_AUG = 128
KNORM_ROWS = SUBLANES
BOUND_SLACK = 1.01
STATIC_BOUND_MAX = 46.0

TM_FFN = 1024
TM_QKV = 1024
TQ_DENSE = 256
TK_DENSE = 256
DENSE_CHUNKS_PER_TRIP = 32
TQ_WIN = 256


def _rms(x, g):
    return (x * lax.rsqrt(jnp.mean(x * x, axis=-1, keepdims=True) + EPS)) * g


def _resident(shape):
    return pl.BlockSpec(shape, lambda *_: (0,) * len(shape), pipeline_mode=pl.Buffered(1))


def _resident_slice(arr, *lead):
    rest = arr.shape[len(lead):]
    return pl.BlockSpec((None,) * len(lead) + rest, lambda *_: lead + (0,) * len(rest),
                        pipeline_mode=pl.Buffered(1))


def _params(*semantics):
    return pltpu.CompilerParams(dimension_semantics=semantics, vmem_limit_bytes=VMEM_LIMIT_BYTES)


def _ffn_half_step(x, gpre_ref, gpost_ref, wg_ref, wu_ref, wd_ref):
    h = _rms(x, gpre_ref[...]).astype(BF16)
    gate = jnp.dot(h, wg_ref[...], preferred_element_type=F32)
    up = jnp.dot(h, wu_ref[...], preferred_element_type=F32)
    act = (gate * jax.nn.sigmoid(gate) * up).astype(BF16)
    y = jnp.dot(act, wd_ref[...], preferred_element_type=F32)
    return x + 0.5 * _rms(y, gpost_ref[...])


def _ffn_kernel(x_ref, gpre_ref, gpost_ref, wg_ref, wu_ref, wd_ref, o_ref):
    o_ref[...] = _ffn_half_step(x_ref[...], gpre_ref, gpost_ref, wg_ref, wu_ref, wd_ref)


def _mixer_out_ffn_kernel(a_ref, x_ref, gmix_ref, wo_ref, gpre_ref, gpost_ref, wg_ref, wu_ref, wd_ref,
                          o_ref):
    m = jnp.dot(a_ref[...], wo_ref[...], preferred_element_type=F32)
    x = x_ref[...] + _rms(m, gmix_ref[...])
    o_ref[...] = _ffn_half_step(x, gpre_ref, gpost_ref, wg_ref, wu_ref, wd_ref)


def _ffn_operands(p, layer, half):
    arrays = (p["norm_g"], p["norm_g"], p["w_gate"], p["w_up"], p["w_down"])
    specs = [_resident_slice(p["norm_g"], layer, NORM_FFN_PRE[half]),
             _resident_slice(p["norm_g"], layer, NORM_FFN_POST[half])]
    specs += [_resident_slice(p[w], layer, half) for w in ("w_gate", "w_up", "w_down")]
    return arrays, specs


def _ffn(x, p, layer):
    n = x.shape[0]
    tm = TM_FFN
    row = pl.BlockSpec((tm, D_MODEL), lambda i: (i, 0))
    arrays, specs = _ffn_operands(p, layer, 0)
    return pl.pallas_call(
        _ffn_kernel,
        grid=(n // tm,),
        in_specs=[row] + specs,
        out_specs=row,
        out_shape=jax.ShapeDtypeStruct((n, D_MODEL), F32),
        compiler_params=_params("parallel"),
        name="ffn",
    )(x, *arrays)


def _mixer_out_ffn(a, x, p, layer):
    n = x.shape[0]
    tm = TM_FFN
    row = pl.BlockSpec((tm, D_MODEL), lambda i: (i, 0))
    arrays, specs = _ffn_operands(p, layer, 1)
    return pl.pallas_call(
        _mixer_out_ffn_kernel,
        grid=(n // tm,),
        in_specs=[row, row, _resident_slice(p["norm_g"], layer, NORM_MIXER_POST),
                  _resident_slice(p["w_o"], layer)] + specs,
        out_specs=row,
        out_shape=jax.ShapeDtypeStruct((n, D_MODEL), F32),
        compiler_params=_params("parallel"),
        name="mixer_out_ffn",
    )(a, x, p["norm_g"], p["w_o"], *arrays)


def _rotate_half(x, axial):
    if axial:
        q = HEAD_DIM // 4
        parts = [x[:, q:2 * q], x[:, 0:q], x[:, 3 * q:4 * q], x[:, 2 * q:3 * q]]
    else:
        h = HEAD_DIM // 2
        parts = [x[:, h:], x[:, :h]]
    return jnp.concatenate(parts, axis=1)


def _head_post(xt, gain_ref, cos, sin, axial):
    if axial:
        ms = jnp.mean(xt * xt, axis=1, keepdims=True)
        xt = (xt * lax.rsqrt(ms + EPS)) * gain_ref[...][None]
    return xt * cos[None] + _rotate_half(xt, axial) * sin[None]


def _with_ones(xt, rows):
    heads, _, n = xt.shape
    pad_shape = (heads, rows - HEAD_DIM, n)
    pad = (lax.broadcasted_iota(jnp.int32, pad_shape, 1) == 0).astype(xt.dtype)
    return jnp.concatenate([xt, pad], axis=1).reshape(heads * rows, n)


def _qkv_kernel(x_ref, g_ref, w_ref, cos_ref, sin_ref, *refs, axial, vch):
    qg_ref, kg_ref = refs[:-5] if axial else (None, None)
    qt_ref, k_ref, vt_ref, qn_ref, kn_ref = refs[-5:]
    tm = x_ref.shape[1]
    h = _rms(x_ref[0], g_ref[...]).astype(BF16)
    qkv = jnp.dot(h, w_ref[...], preferred_element_type=F32)
    cos = cos_ref[...]
    sin = sin_ref[...]

    qt = qkv[:, :Q_DIM].T.reshape(N_HEADS, HEAD_DIM, tm)
    qt = _head_post(qt, qg_ref, cos, sin, axial) * (HEAD_DIM ** -0.5 * LOG2E)
    qt_ref[0] = qt.reshape(Q_DIM, tm).astype(BF16)
    qn_ref[0] = jnp.sqrt(jnp.sum(qt * qt, axis=1))

    kt = qkv[:, Q_DIM:Q_DIM + KV_DIM].T.reshape(N_KV_HEADS, HEAD_DIM, tm)
    kt = _with_ones(_head_post(kt, kg_ref, cos, sin, axial), K_AUG)
    for kh in range(N_KV_HEADS):
        k_ref[0, kh] = kt[kh * K_AUG:(kh + 1) * K_AUG].T.astype(BF16)
    kf = kt.reshape(N_KV_HEADS, K_AUG, tm)
    kn = jnp.sqrt(jnp.sum(kf * kf, axis=1))
    kn_ref[0] = jnp.concatenate([kn, jnp.zeros((KNORM_ROWS - N_KV_HEADS, tm), F32)], axis=0)

    v = qkv[:, Q_DIM + KV_DIM:]
    for j in range(tm // vch):
        vt_ref[0, j] = v[j * vch:(j + 1) * vch, :].T.astype(BF16)


def _qkv(x, p, layer, cos_t, sin_t, *, axial, vch):
    b, t, _ = x.shape
    tm = TM_QKV
    v_rows = KV_DIM
    kern = functools.partial(_qkv_kernel, axial=axial, vch=vch)
    gains = (p["q_norm"], p["k_norm"]) if axial else ()
    return pl.pallas_call(
        kern,
        grid=(b, t // tm),
        in_specs=[pl.BlockSpec((1, tm, D_MODEL), lambda bi, i: (bi, i, 0)),
                  _resident_slice(p["norm_g"], layer, NORM_MIXER_PRE),
                  _resident_slice(p["w_qkv"], layer),
                  pl.BlockSpec((HEAD_DIM, tm), lambda bi, i: (0, i)),
                  pl.BlockSpec((HEAD_DIM, tm), lambda bi, i: (0, i))]
                 + [_resident_slice(gain, layer // 2) for gain in gains],
        out_specs=[pl.BlockSpec((1, Q_DIM, tm), lambda bi, i: (bi, 0, i)),
                   pl.BlockSpec((1, N_KV_HEADS, tm, K_AUG), lambda bi, i: (bi, 0, i, 0)),
                   pl.BlockSpec((1, tm // vch, v_rows, vch), lambda bi, i: (bi, i, 0, 0)),
                   pl.BlockSpec((1, N_HEADS, tm), lambda bi, i: (bi, 0, i)),
                   pl.BlockSpec((1, KNORM_ROWS, tm), lambda bi, i: (bi, 0, i))],
        out_shape=[jax.ShapeDtypeStruct((b, Q_DIM, t), BF16),
                   jax.ShapeDtypeStruct((b, N_KV_HEADS, t, K_AUG), BF16),
                   jax.ShapeDtypeStruct((b, t // vch, v_rows, vch), BF16),
                   jax.ShapeDtypeStruct((b, N_HEADS, t), F32),
                   jax.ShapeDtypeStruct((b, KNORM_ROWS, t), F32)],
        compiler_params=_params("parallel", "parallel"),
        name="qkv_axial" if axial else "qkv_window",
    )(x, p["norm_g"], p["w_qkv"], cos_t, sin_t, *gains)


def _store_heads(o_ref, ot, kh, tq):
    stacked = jnp.concatenate([ot[:, g * tq:(g + 1) * tq] for g in range(GROUP)], axis=0)
    width = GROUP * HEAD_DIM
    o_ref[0, :, kh * width:(kh + 1) * width] = stacked.T.astype(o_ref.dtype)


def _batch_score_check(qn_all_ref, kn_all_ref, kmax_ref, ok_ref):
    @pl.when(pl.program_id(1) == 0)
    def _():
        kmax = jnp.max(kn_all_ref[0], axis=1, keepdims=True)
        qmax = jnp.max(qn_all_ref[0], axis=1, keepdims=True)
        kmax_per_head = jnp.concatenate(
            [jnp.broadcast_to(kmax[kh:kh + 1, :], (GROUP, 1)) for kh in range(N_KV_HEADS)], axis=0)
        worst = jnp.max(qmax * kmax_per_head) * BOUND_SLACK
        kmax_ref[...] = kmax
        ok_ref[0] = (worst <= STATIC_BOUND_MAX).astype(jnp.int32)


def _batch_check_scratch():
    return [pltpu.VMEM((KNORM_ROWS, 1), F32), pltpu.SMEM((1,), jnp.int32)]


def _score_bounds(qn_ref, kmax_ref):
    bounds = []
    for kh in range(N_KV_HEADS):
        qnorm = jnp.concatenate([qn_ref[0, kh * GROUP + g:kh * GROUP + g + 1, :] for g in range(GROUP)],
                                axis=1)
        bounds.append(qnorm * (kmax_ref[kh:kh + 1, :] * BOUND_SLACK))
    return bounds


def _fill_query_block(qaug, qt_ref, kh, offset, tq):
    for g in range(GROUP):
        h = kh * GROUP + g
        qaug[0:HEAD_DIM, g * tq:(g + 1) * tq] = qt_ref[0, h * HEAD_DIM:(h + 1) * HEAD_DIM, :]
    tail_rows = lax.broadcasted_iota(jnp.int32, (K_AUG - HEAD_DIM, GROUP * tq), 0)
    qaug[HEAD_DIM:, :] = jnp.where(tail_rows == 0, offset, 0.0).astype(BF16)


def _dense_attn_kernel(qt_ref, k_ref, vt_ref, qn_ref, qn_all_ref, kn_all_ref, o_ref, qaug_ref, pa_ref, pb_ref,
                       acc_ref, l_ref, kmax_ref, ok_ref, *, tq, tk):
    t = k_ref.shape[2]
    n = GROUP * tq
    nc = t // tk

    _batch_score_check(qn_all_ref, kn_all_ref, kmax_ref, ok_ref)
    static_ok = ok_ref[0] == 1
    bounds = _score_bounds(qn_ref, kmax_ref)
    for kh in range(N_KV_HEADS):
        _fill_query_block(qaug_ref.at[kh], qt_ref, kh, jnp.where(static_ok, -bounds[kh], 0.0), tq)
    acc_ref[...] = jnp.zeros(acc_ref.shape, F32)
    l_ref[...] = jnp.zeros(l_ref.shape, F32)

    def rowsum8(p):
        return jnp.sum(p.reshape(tk // SUBLANES, SUBLANES, n), axis=0)

    def scores(kh, c):
        off = pl.multiple_of(c * tk, tk)
        return jnp.dot(k_ref[0, kh, pl.ds(off, tk), :], qaug_ref[kh],
                       preferred_element_type=F32)

    def consume(kh, c, src):
        rows = pl.ds(pl.multiple_of(kh * HEAD_DIM, HEAD_DIM), HEAD_DIM)
        acc_ref[kh] += jnp.dot(vt_ref[0, c, rows, :], src[...], preferred_element_type=F32)

    @pl.when(static_ok)
    def _():
        bufs = (pa_ref, pb_ref)
        unroll = DENSE_CHUNKS_PER_TRIP
        steps = N_KV_HEADS * nc

        def produce(f, dst):
            kh, c = f // nc, f % nc
            p = jnp.exp2(scores(kh, c))
            l_ref[kh] += rowsum8(p)
            dst[...] = p.astype(BF16)

        def trip(first, last_trip):
            for j in range(unroll):
                f = first + j
                if not (last_trip and j == unroll - 1):
                    produce(f + 1, bufs[(j + 1) % 2])
                consume(f // nc, f % nc, bufs[j % 2])

        def body(tr, carry):
            trip(tr * unroll, False)
            return carry

        produce(0, pa_ref)
        lax.fori_loop(0, steps // unroll - 1, body, 0)
        trip(steps - unroll, True)

    @pl.when(jnp.logical_not(static_ok))
    def _():
        for kh in range(N_KV_HEADS):
            def chunk(c, m, kh=kh):
                s = scores(kh, c)
                m_new = jnp.maximum(m, jnp.max(s, axis=0, keepdims=True))
                alpha = jnp.exp2(m - m_new)
                p = jnp.exp2(s - m_new)
                l_ref[kh] = alpha * l_ref[kh] + rowsum8(p)
                pa_ref[...] = p.astype(BF16)
                acc_ref[kh] *= alpha
                consume(kh, c, pa_ref)
                return m_new

            lax.fori_loop(0, nc, chunk, jnp.full((1, n), -jnp.inf, F32))

    for kh in range(N_KV_HEADS):
        denom = jnp.sum(l_ref[kh], axis=0, keepdims=True)
        _store_heads(o_ref, acc_ref[kh] * (1.0 / denom), kh, tq)


def _dense_attention(qt, k, vt, qn, kn):
    b, _, t = qt.shape
    tq, tk = TQ_DENSE, TK_DENSE
    n = GROUP * tq
    kern = functools.partial(_dense_attn_kernel, tq=tq, tk=tk)
    return pl.pallas_call(
        kern,
        grid=(b, t // tq),
        in_specs=[pl.BlockSpec((1, Q_DIM, tq), lambda bi, i: (bi, 0, i)),
                  pl.BlockSpec((1, N_KV_HEADS, t, K_AUG), lambda bi, i: (bi, 0, 0, 0)),
                  pl.BlockSpec((1, t // tk, KV_DIM, tk), lambda bi, i: (bi, 0, 0, 0)),
                  pl.BlockSpec((1, N_HEADS, tq), lambda bi, i: (bi, 0, i)),
                  pl.BlockSpec((1, N_HEADS, t), lambda bi, i: (bi, 0, 0)),
                  pl.BlockSpec((1, KNORM_ROWS, t), lambda bi, i: (bi, 0, 0))],
        out_specs=pl.BlockSpec((1, tq, Q_DIM), lambda bi, i: (bi, i, 0)),
        out_shape=jax.ShapeDtypeStruct((b, t, Q_DIM), BF16),
        scratch_shapes=[pltpu.VMEM((N_KV_HEADS, K_AUG, n), BF16),
                        pltpu.VMEM((tk, n), BF16), pltpu.VMEM((tk, n), BF16),
                        pltpu.VMEM((N_KV_HEADS, HEAD_DIM, n), F32),
                        pltpu.VMEM((N_KV_HEADS, SUBLANES, n), F32)] + _batch_check_scratch(),
        compiler_params=_params("parallel", "arbitrary"),
        name="attn_dense",
    )(qt, k, vt, qn, qn, kn)


def _window_attn_kernel(qt_ref, kp_ref, kc_ref, kx_ref, vp_ref, vc_ref, vx_ref, qn_ref, qn_all_ref, kn_all_ref,
                        bias_ref, sink_ref, o_ref, qaug_ref, p_ref, kmax_ref, ok_ref, *, tq):
    i = pl.program_id(1)
    last = pl.num_programs(1) - 1
    w = WINDOW
    _batch_score_check(qn_all_ref, kn_all_ref, kmax_ref, ok_ref)
    static_ok = ok_ref[0] == 1
    bounds = _score_bounds(qn_ref, kmax_ref)

    def produce(kh, static):
        offset = -bounds[kh] if static else jnp.zeros_like(bounds[kh])
        qaug = qaug_ref.at[kh]
        _fill_query_block(qaug, qt_ref, kh, offset, tq)
        keys = jnp.concatenate([kp_ref[0, kh], kc_ref[0, kh], kx_ref[0, kh]], axis=0)
        bias = jnp.concatenate([
            jnp.where(i > 0, bias_ref[:w], MASKED_SCORE),
            bias_ref[w:w + tq],
            jnp.where(i < last, bias_ref[w + tq:], MASKED_SCORE)], axis=0)
        s = jnp.dot(keys, qaug[...], preferred_element_type=F32) + bias
        sink = sink_ref[kh]
        if static:
            p = jnp.exp2(s)
            sink_term = jnp.exp2(sink + offset.astype(BF16).astype(F32))
        else:
            m = jnp.maximum(jnp.max(s, axis=0, keepdims=True), sink)
            p = jnp.exp2(s - m)
            sink_term = jnp.exp2(sink - m)
        p_ref[kh % 2] = p.astype(BF16)
        return jnp.sum(p, axis=0, keepdims=True) + sink_term

    def consume(kh, denom):
        rows = slice(kh * HEAD_DIM, (kh + 1) * HEAD_DIM)
        vals = jnp.concatenate([vp_ref[0, 0, rows, :]]
                               + [vc_ref[0, j, rows, :] for j in range(tq // w)]
                               + [vx_ref[0, 0, rows, :]], axis=1)
        ot = jnp.dot(vals, p_ref[kh % 2], preferred_element_type=F32)
        _store_heads(o_ref, ot * (1.0 / denom), kh, tq)

    def heads(static):
        denom = produce(0, static)
        for kh in range(N_KV_HEADS):
            next_denom = produce(kh + 1, static) if kh + 1 < N_KV_HEADS else None
            consume(kh, denom)
            denom = next_denom

    @pl.when(static_ok)
    def _():
        heads(True)

    @pl.when(jnp.logical_not(static_ok))
    def _():
        heads(False)


def _window_bias(tq):
    span = tq + 2 * WINDOW
    key_rel = jnp.arange(span)[:, None] - WINDOW
    q_rel = (jnp.arange(GROUP * tq) % tq)[None, :]
    return jnp.where(jnp.abs(key_rel - q_rel) <= WINDOW, 0.0, MASKED_SCORE).astype(F32)


def _window_attention(qt, k, vt, qn, kn, sink):
    b, _, t = qt.shape
    tq, w = TQ_WIN, WINDOW
    r = tq // w
    nblk = t // w
    n = GROUP * tq
    span = tq + 2 * w
    sink_rows = jnp.repeat((sink.astype(F32) * LOG2E).reshape(N_KV_HEADS, 1, GROUP), tq, axis=2)
    prev = lambda i: jnp.maximum(i * r - 1, 0)
    nxt = lambda i: jnp.minimum(i * r + r, nblk - 1)
    kern = functools.partial(_window_attn_kernel, tq=tq)
    return pl.pallas_call(
        kern,
        grid=(b, t // tq),
        in_specs=[pl.BlockSpec((1, Q_DIM, tq), lambda bi, i: (bi, 0, i)),
                  pl.BlockSpec((1, N_KV_HEADS, w, K_AUG), lambda bi, i: (bi, 0, prev(i), 0)),
                  pl.BlockSpec((1, N_KV_HEADS, tq, K_AUG), lambda bi, i: (bi, 0, i, 0)),
                  pl.BlockSpec((1, N_KV_HEADS, w, K_AUG), lambda bi, i: (bi, 0, nxt(i), 0)),
                  pl.BlockSpec((1, 1, KV_DIM, w), lambda bi, i: (bi, prev(i), 0, 0)),
                  pl.BlockSpec((1, r, KV_DIM, w), lambda bi, i: (bi, i, 0, 0)),
                  pl.BlockSpec((1, 1, KV_DIM, w), lambda bi, i: (bi, nxt(i), 0, 0)),
                  pl.BlockSpec((1, N_HEADS, tq), lambda bi, i: (bi, 0, i)),
                  pl.BlockSpec((1, N_HEADS, t), lambda bi, i: (bi, 0, 0)),
                  pl.BlockSpec((1, KNORM_ROWS, t), lambda bi, i: (bi, 0, 0)),
                  _resident((span, n)),
                  _resident((N_KV_HEADS, 1, n))],
        out_specs=pl.BlockSpec((1, tq, Q_DIM), lambda bi, i: (bi, i, 0)),
        out_shape=jax.ShapeDtypeStruct((b, t, Q_DIM), BF16),
        scratch_shapes=[pltpu.VMEM((N_KV_HEADS, K_AUG, n), BF16),
                        pltpu.VMEM((2, span, n), BF16)] + _batch_check_scratch(),
        compiler_params=_params("parallel", "arbitrary"),
        name="attn_window",
    )(qt, k, k, k, vt, vt, vt, qn, qn, kn, _window_bias(tq), sink_rows)


def _signed_tables(ang):
    cos = jnp.cos(ang)
    sin = jnp.sin(ang)
    return jnp.concatenate([cos, cos], axis=1), jnp.concatenate([-sin, sin], axis=1)


def _window_tables(t):
    half = HEAD_DIM // 2
    inv_freq = ROPE_THETA ** (-jnp.arange(half, dtype=F32) / half)
    pos = jnp.arange(t, dtype=F32)
    cos, sin = _signed_tables(pos[:, None] * inv_freq[None, :])
    return cos.T, sin.T


def _axial_tables(t):
    half = HEAD_DIM // 4
    rows = t // GRID_W
    inv_freq = ROPE_THETA ** (-jnp.arange(half, dtype=F32) / half)
    row = jnp.repeat(jnp.arange(rows, dtype=F32), GRID_W)
    col = jnp.tile(jnp.arange(GRID_W, dtype=F32), rows)
    cos_r, sin_r = _signed_tables(row[:, None] * inv_freq[None, :])
    cos_c, sin_c = _signed_tables(col[:, None] * inv_freq[None, :])
    return (jnp.concatenate([cos_r, cos_c], axis=1).T,
            jnp.concatenate([sin_r, sin_c], axis=1).T)


def _encoder(x, p):
    b, t, d = x.shape
    tables = (_window_tables(t), _axial_tables(t))
    x = x.reshape(b * t, d)
    for layer in range(DEPTH):
        x = _ffn(x, p, layer)
        axial = layer % 2 == 1
        cos_t, sin_t = tables[layer % 2]
        if axial:
            qt, k, vt, qn, kn = _qkv(x.reshape(b, t, d), p, layer, cos_t, sin_t, axial=True, vch=TK_DENSE)
            a = _dense_attention(qt, k, vt, qn, kn)
        else:
            qt, k, vt, qn, kn = _qkv(x.reshape(b, t, d), p, layer, cos_t, sin_t, axial=False, vch=WINDOW)
            a = _window_attention(qt, k, vt, qn, kn, p["attn_sink"][layer // 2])
        x = _mixer_out_ffn(a.reshape(b * t, Q_DIM), x, p, layer)
    return x.reshape(b, t, d)


def kernel(x_prompt, x_sample, norm_g, w_qkv, w_o, attn_sink, q_norm, k_norm, w_gate, w_up, w_down):
    p = {"norm_g": norm_g.reshape(DEPTH, -1, 1, D_MODEL),
         "q_norm": q_norm.reshape(-1, HEAD_DIM, 1), "k_norm": k_norm.reshape(-1, HEAD_DIM, 1),
         "attn_sink": attn_sink,
         "w_qkv": w_qkv.astype(BF16), "w_o": w_o.astype(BF16),
         "w_gate": w_gate.astype(BF16), "w_up": w_up.astype(BF16), "w_down": w_down.astype(BF16)}
    return (_encoder(x_prompt, p), _encoder(x_sample, p))
```

```python
import functools

import jax
import jax.numpy as jnp
from jax import lax
from jax.experimental import pallas as pl
from jax.experimental.pallas import tpu as pltpu

D_MODEL = 1024
DEPTH = 4
N_HEADS = 16
N_KV_HEADS = 4
HEAD_DIM = D_MODEL // N_HEADS
GROUP = N_HEADS // N_KV_HEADS
Q_DIM = N_HEADS * HEAD_DIM
KV_DIM = N_KV_HEADS * HEAD_DIM
QKV_DIM = Q_DIM + 2 * KV_DIM
D_FF = 2816
WINDOW = 128
GRID_W = 64
ROPE_THETA = 10000.0
EPS = 1e-6
NORM_FFN_PRE, NORM_FFN_POST = (0, 4), (1, 5)
NORM_MIXER_PRE, NORM_MIXER_POST = 2, 3

BF16 = jnp.bfloat16
F32 = jnp.float32

MASKED_SCORE = -1e30
LOG2E = 1.4426950408889634
VMEM_LIMIT_BYTES = 56 * 1024 * 1024

SUBLANES = 8
K_AUG = 128
KNORM_ROWS = SUBLANES
BOUND_SLACK = 1.01
STATIC_BOUND_MAX = 46.0

TM_FFN = 1024
TM_QKV = 1024
TQ_DENSE = 256
TK_DENSE = 256
DENSE_CHUNKS_PER_TRIP = 32
TQ_WIN = 256


def _rms(x, g):
    return (x * lax.rsqrt(jnp.mean(x * x, axis=-1, keepdims=True) + EPS)) * g


def _resident(shape):
    return pl.BlockSpec(shape, lambda *_: (0,) * len(shape), pipeline_mode=pl.Buffered(1))


def _resident_slice(arr, *lead):
    rest = arr.shape[len(lead):]
    return pl.BlockSpec((None,) * len(lead) + rest, lambda *_: lead + (0,) * len(rest),
                        pipeline_mode=pl.Buffered(1))


def _params(*semantics):
    return pltpu.CompilerParams(dimension_semantics=semantics, vmem_limit_bytes=VMEM_LIMIT_BYTES)


def _ffn_half_step(x, gpre_ref, gpost_ref, wg_ref, wu_ref, wd_ref):
    h = _rms(x, gpre_ref[...]).astype(BF16)
    gate = jnp.dot(h, wg_ref[...], preferred_element_type=F32)
    up = jnp.dot(h, wu_ref[...], preferred_element_type=F32)
    act = (gate * jax.nn.sigmoid(gate) * up).astype(BF16)
    y = jnp.dot(act, wd_ref[...], preferred_element_type=F32)
    return x + 0.5 * _rms(y, gpost_ref[...])


def _ffn_kernel(x_ref, gpre_ref, gpost_ref, wg_ref, wu_ref, wd_ref, o_ref):
    o_ref[...] = _ffn_half_step(x_ref[...], gpre_ref, gpost_ref, wg_ref, wu_ref, wd_ref)


def _mixer_out_ffn_kernel(a_ref, x_ref, gmix_ref, wo_ref, gpre_ref, gpost_ref, wg_ref, wu_ref, wd_ref,
                          o_ref):
    m = jnp.dot(a_ref[...], wo_ref[...], preferred_element_type=F32)
    x = x_ref[...] + _rms(m, gmix_ref[...])
    o_ref[...] = _ffn_half_step(x, gpre_ref, gpost_ref, wg_ref, wu_ref, wd_ref)


def _ffn_operands(p, layer, half):
    arrays = (p["norm_g"], p["norm_g"], p["w_gate"], p["w_up"], p["w_down"])
    specs = [_resident_slice(p["norm_g"], layer, NORM_FFN_PRE[half]),
             _resident_slice(p["norm_g"], layer, NORM_FFN_POST[half])]
    specs += [_resident_slice(p[w], layer, half) for w in ("w_gate", "w_up", "w_down")]
    return arrays, specs


def _ffn(x, p, layer):
    n = x.shape[0]
    tm = TM_FFN
    row = pl.BlockSpec((tm, D_MODEL), lambda i: (i, 0))
    arrays, specs = _ffn_operands(p, layer, 0)
    return pl.pallas_call(
        _ffn_kernel,
        grid=(n // tm,),
        in_specs=[row] + specs,
        out_specs=row,
        out_shape=jax.ShapeDtypeStruct((n, D_MODEL), F32),
        compiler_params=_params("parallel"),
        name="ffn",
    )(x, *arrays)


def _mixer_out_ffn(a, x, p, layer):
    n = x.shape[0]
    tm = TM_FFN
    row = pl.BlockSpec((tm, D_MODEL), lambda i: (i, 0))
    arrays, specs = _ffn_operands(p, layer, 1)
    return pl.pallas_call(
        _mixer_out_ffn_kernel,
        grid=(n // tm,),
        in_specs=[row, row, _resident_slice(p["norm_g"], layer, NORM_MIXER_POST),
                  _resident_slice(p["w_o"], layer)] + specs,
        out_specs=row,
        out_shape=jax.ShapeDtypeStruct((n, D_MODEL), F32),
        compiler_params=_params("parallel"),
        name="mixer_out_ffn",
    )(a, x, p["norm_g"], p["w_o"], *arrays)


def _rotate_half(x, axial):
    if axial:
        q = HEAD_DIM // 4
        parts = [x[:, q:2 * q], x[:, 0:q], x[:, 3 * q:4 * q], x[:, 2 * q:3 * q]]
    else:
        h = HEAD_DIM // 2
        parts = [x[:, h:], x[:, :h]]
    return jnp.concatenate(parts, axis=1)


def _head_post(xt, gain_ref, cos, sin, axial):
    if axial:
        ms = jnp.mean(xt * xt, axis=1, keepdims=True)
        xt = (xt * lax.rsqrt(ms + EPS)) * gain_ref[...][None]
    return xt * cos[None] + _rotate_half(xt, axial) * sin[None]


def _with_ones(xt, rows):
    heads, _, n = xt.shape
    pad_shape = (heads, rows - HEAD_DIM, n)
    pad = (lax.broadcasted_iota(jnp.int32, pad_shape, 1) == 0).astype(xt.dtype)
    return jnp.concatenate([xt, pad], axis=1).reshape(heads * rows, n)


def _qkv_kernel(x_ref, g_ref, w_ref, cos_ref, sin_ref, *refs, axial, vch):
    qg_ref, kg_ref = refs[:-5] if axial else (None, None)
    qt_ref, k_ref, vt_ref, qn_ref, kn_ref = refs[-5:]
    tm = x_ref.shape[1]
    h = _rms(x_ref[0], g_ref[...]).astype(BF16)
    qkv = jnp.dot(h, w_ref[...], preferred_element_type=F32)
    cos = cos_ref[...]
    sin = sin_ref[...]

    qt = qkv[:, :Q_DIM].T.reshape(N_HEADS, HEAD_DIM, tm)
    qt = _head_post(qt, qg_ref, cos, sin, axial) * (HEAD_DIM ** -0.5 * LOG2E)
    qt_ref[0] = qt.reshape(Q_DIM, tm).astype(BF16)
    qn_ref[0] = jnp.sqrt(jnp.sum(qt * qt, axis=1))

    kt = qkv[:, Q_DIM:Q_DIM + KV_DIM].T.reshape(N_KV_HEADS, HEAD_DIM, tm)
    kt = _with_ones(_head_post(kt, kg_ref, cos, sin, axial), K_AUG)
    for kh in range(N_KV_HEADS):
        k_ref[0, kh] = kt[kh * K_AUG:(kh + 1) * K_AUG].T.astype(BF16)
    kf = kt.reshape(N_KV_HEADS, K_AUG, tm)
    kn = jnp.sqrt(jnp.sum(kf * kf, axis=1))
    kn_ref[0] = jnp.concatenate([kn, jnp.zeros((KNORM_ROWS - N_KV_HEADS, tm), F32)], axis=0)

    v = qkv[:, Q_DIM + KV_DIM:]
    for j in range(tm // vch):
        vt_ref[0, j] = v[j * vch:(j + 1) * vch, :].T.astype(BF16)


def _qkv(x, p, layer, cos_t, sin_t, *, axial, vch):
    b, t, _ = x.shape
    tm = TM_QKV
    v_rows = KV_DIM
    kern = functools.partial(_qkv_kernel, axial=axial, vch=vch)
    gains = (p["q_norm"], p["k_norm"]) if axial else ()
    return pl.pallas_call(
        kern,
        grid=(b, t // tm),
        in_specs=[pl.BlockSpec((1, tm, D_MODEL), lambda bi, i: (bi, i, 0)),
                  _resident_slice(p["norm_g"], layer, NORM_MIXER_PRE),
                  _resident_slice(p["w_qkv"], layer),
                  pl.BlockSpec((HEAD_DIM, tm), lambda bi, i: (0, i)),
                  pl.BlockSpec((HEAD_DIM, tm), lambda bi, i: (0, i))]
                 + [_resident_slice(gain, layer // 2) for gain in gains],
        out_specs=[pl.BlockSpec((1, Q_DIM, tm), lambda bi, i: (bi, 0, i)),
                   pl.BlockSpec((1, N_KV_HEADS, tm, K_AUG), lambda bi, i: (bi, 0, i, 0)),
                   pl.BlockSpec((1, tm // vch, v_rows, vch), lambda bi, i: (bi, i, 0, 0)),
                   pl.BlockSpec((1, N_HEADS, tm), lambda bi, i: (bi, 0, i)),
                   pl.BlockSpec((1, KNORM_ROWS, tm), lambda bi, i: (bi, 0, i))],
        out_shape=[jax.ShapeDtypeStruct((b, Q_DIM, t), BF16),
                   jax.ShapeDtypeStruct((b, N_KV_HEADS, t, K_AUG), BF16),
                   jax.ShapeDtypeStruct((b, t // vch, v_rows, vch), BF16),
                   jax.ShapeDtypeStruct((b, N_HEADS, t), F32),
                   jax.ShapeDtypeStruct((b, KNORM_ROWS, t), F32)],
        compiler_params=_params("parallel", "parallel"),
        name="qkv_axial" if axial else "qkv_window",
    )(x, p["norm_g"], p["w_qkv"], cos_t, sin_t, *gains)


def _store_heads(o_ref, ot, kh, tq):
    stacked = jnp.concatenate([ot[:, g * tq:(g + 1) * tq] for g in range(GROUP)], axis=0)
    width = GROUP * HEAD_DIM
    o_ref[0, :, kh * width:(kh + 1) * width] = stacked.T.astype(o_ref.dtype)


def _batch_score_check(qn_all_ref, kn_all_ref, kmax_ref, ok_ref):
    @pl.when(pl.program_id(1) == 0)
    def _():
        kmax = jnp.max(kn_all_ref[0], axis=1, keepdims=True)
        qmax = jnp.max(qn_all_ref[0], axis=1, keepdims=True)
        kmax_per_head = jnp.concatenate(
            [jnp.broadcast_to(kmax[kh:kh + 1, :], (GROUP, 1)) for kh in range(N_KV_HEADS)], axis=0)
        worst = jnp.max(qmax * kmax_per_head) * BOUND_SLACK
        kmax_ref[...] = kmax
        ok_ref[0] = (worst <= STATIC_BOUND_MAX).astype(jnp.int32)


def _batch_check_scratch():
    return [pltpu.VMEM((KNORM_ROWS, 1), F32), pltpu.SMEM((1,), jnp.int32)]


def _score_bounds(qn_ref, kmax_ref):
    bounds = []
    for kh in range(N_KV_HEADS):
        qnorm = jnp.concatenate([qn_ref[0, kh * GROUP + g:kh * GROUP + g + 1, :] for g in range(GROUP)],
                                axis=1)
        bounds.append(qnorm * (kmax_ref[kh:kh + 1, :] * BOUND_SLACK))
    return bounds


def _fill_query_block(qaug, qt_ref, kh, offset, tq):
    for g in range(GROUP):
        h = kh * GROUP + g
        qaug[0:HEAD_DIM, g * tq:(g + 1) * tq] = qt_ref[0, h * HEAD_DIM:(h + 1) * HEAD_DIM, :]
    tail_rows = lax.broadcasted_iota(jnp.int32, (K_AUG - HEAD_DIM, GROUP * tq), 0)
    qaug[HEAD_DIM:, :] = jnp.where(tail_rows == 0, offset, 0.0).astype(BF16)


def _dense_attn_kernel(qt_ref, k_ref, vt_ref, qn_ref, qn_all_ref, kn_all_ref, o_ref, qaug_ref, pa_ref, pb_ref,
                       acc_ref, l_ref, kmax_ref, ok_ref, *, tq, tk):
    t = k_ref.shape[2]
    n = GROUP * tq
    nc = t // tk

    _batch_score_check(qn_all_ref, kn_all_ref, kmax_ref, ok_ref)
    static_ok = ok_ref[0] == 1
    bounds = _score_bounds(qn_ref, kmax_ref)
    def fill_query_block(kh):
        _fill_query_block(qaug_ref.at[kh], qt_ref, kh, jnp.where(static_ok, -bounds[kh], 0.0), tq)

    def finish_setup():
        for kh in range(1, N_KV_HEADS):
            fill_query_block(kh)
        acc_ref[...] = jnp.zeros(acc_ref.shape, F32)

    fill_query_block(0)
    l_ref[...] = jnp.zeros(l_ref.shape, F32)

    def rowsum8(p):
        return jnp.sum(p.reshape(tk // SUBLANES, SUBLANES, n), axis=0)

    def scores(kh, c):
        off = pl.multiple_of(c * tk, tk)
        return jnp.dot(k_ref[0, kh, pl.ds(off, tk), :], qaug_ref[kh],
                       preferred_element_type=F32)

    def consume(kh, c, src):
        rows = pl.ds(pl.multiple_of(kh * HEAD_DIM, HEAD_DIM), HEAD_DIM)
        acc_ref[kh] += jnp.dot(vt_ref[0, c, rows, :], src[...], preferred_element_type=F32)

    @pl.when(static_ok)
    def _():
        bufs = (pa_ref, pb_ref)
        unroll = DENSE_CHUNKS_PER_TRIP
        steps = N_KV_HEADS * nc

        def produce(f, dst):
            kh, c = f // nc, f % nc
            p = jnp.exp2(scores(kh, c))
            l_ref[kh] += rowsum8(p)
            dst[...] = p.astype(BF16)

        def trip(first, last_trip):
            for j in range(unroll):
                f = first + j
                if not (last_trip and j == unroll - 1):
                    produce(f + 1, bufs[(j + 1) % 2])
                consume(f // nc, f % nc, bufs[j % 2])

        def body(tr, carry):
            trip(tr * unroll, False)
            return carry

        produce(0, pa_ref)
        finish_setup()
        lax.fori_loop(0, steps // unroll - 1, body, 0)
        trip(steps - unroll, True)

    @pl.when(jnp.logical_not(static_ok))
    def _():
        finish_setup()
        for kh in range(N_KV_HEADS):
            def chunk(c, m, kh=kh):
                s = scores(kh, c)
                m_new = jnp.maximum(m, jnp.max(s, axis=0, keepdims=True))
                alpha = jnp.exp2(m - m_new)
                p = jnp.exp2(s - m_new)
                l_ref[kh] = alpha * l_ref[kh] + rowsum8(p)
                pa_ref[...] = p.astype(BF16)
                acc_ref[kh] *= alpha
                consume(kh, c, pa_ref)
                return m_new

            lax.fori_loop(0, nc, chunk, jnp.full((1, n), -jnp.inf, F32))

    for kh in range(N_KV_HEADS):
        denom = jnp.sum(l_ref[kh], axis=0, keepdims=True)
        _store_heads(o_ref, acc_ref[kh] * (1.0 / denom), kh, tq)


def _dense_attention(qt, k, vt, qn, kn):
    b, _, t = qt.shape
    tq, tk = TQ_DENSE, TK_DENSE
    n = GROUP * tq
    kern = functools.partial(_dense_attn_kernel, tq=tq, tk=tk)
    return pl.pallas_call(
        kern,
        grid=(b, t // tq),
        in_specs=[pl.BlockSpec((1, Q_DIM, tq), lambda bi, i: (bi, 0, i)),
                  pl.BlockSpec((1, N_KV_HEADS, t, K_AUG), lambda bi, i: (bi, 0, 0, 0)),
                  pl.BlockSpec((1, t // tk, KV_DIM, tk), lambda bi, i: (bi, 0, 0, 0)),
                  pl.BlockSpec((1, N_HEADS, tq), lambda bi, i: (bi, 0, i)),
                  pl.BlockSpec((1, N_HEADS, t), lambda bi, i: (bi, 0, 0)),
                  pl.BlockSpec((1, KNORM_ROWS, t), lambda bi, i: (bi, 0, 0))],
        out_specs=pl.BlockSpec((1, tq, Q_DIM), lambda bi, i: (bi, i, 0)),
        out_shape=jax.ShapeDtypeStruct((b, t, Q_DIM), BF16),
        scratch_shapes=[pltpu.VMEM((N_KV_HEADS, K_AUG, n), BF16),
                        pltpu.VMEM((tk, n), BF16), pltpu.VMEM((tk, n), BF16),
                        pltpu.VMEM((N_KV_HEADS, HEAD_DIM, n), F32),
                        pltpu.VMEM((N_KV_HEADS, SUBLANES, n), F32)] + _batch_check_scratch(),
        compiler_params=_params("parallel", "arbitrary"),
        name="attn_dense",
    )(qt, k, vt, qn, qn, kn)


def _window_attn_kernel(qt_ref, kp_ref, kc_ref, kx_ref, vp_ref, vc_ref, vx_ref, qn_ref, qn_all_ref, kn_all_ref,
                        bias_ref, sink_ref, o_ref, qaug_ref, p_ref, kmax_ref, ok_ref, *, tq):
    i = pl.program_id(1)
    last = pl.num_programs(1) - 1
    w = WINDOW
    _batch_score_check(qn_all_ref, kn_all_ref, kmax_ref, ok_ref)
    static_ok = ok_ref[0] == 1
    bounds = _score_bounds(qn_ref, kmax_ref)

    def produce(kh, static):
        offset = -bounds[kh] if static else jnp.zeros_like(bounds[kh])
        qaug = qaug_ref.at[kh]
        _fill_query_block(qaug, qt_ref, kh, offset, tq)
        keys = jnp.concatenate([kp_ref[0, kh], kc_ref[0, kh], kx_ref[0, kh]], axis=0)
        bias = jnp.concatenate([
            jnp.where(i > 0, bias_ref[:w], MASKED_SCORE),
            bias_ref[w:w + tq],
            jnp.where(i < last, bias_ref[w + tq:], MASKED_SCORE)], axis=0)
        s = jnp.dot(keys, qaug[...], preferred_element_type=F32) + bias
        sink = sink_ref[kh]
        if static:
            p = jnp.exp2(s)
            sink_term = jnp.exp2(sink + offset.astype(BF16).astype(F32))
        else:
            m = jnp.maximum(jnp.max(s, axis=0, keepdims=True), sink)
            p = jnp.exp2(s - m)
            sink_term = jnp.exp2(sink - m)
        p_ref[kh % 2] = p.astype(BF16)
        return jnp.sum(p, axis=0, keepdims=True) + sink_term

    def consume(kh, denom):
        rows = slice(kh * HEAD_DIM, (kh + 1) * HEAD_DIM)
        vals = jnp.concatenate([vp_ref[0, 0, rows, :]]
                               + [vc_ref[0, j, rows, :] for j in range(tq // w)]
                               + [vx_ref[0, 0, rows, :]], axis=1)
        ot = jnp.dot(vals, p_ref[kh % 2], preferred_element_type=F32)
        _store_heads(o_ref, ot * (1.0 / denom), kh, tq)

    def heads(static):
        denom = produce(0, static)
        for kh in range(N_KV_HEADS):
            next_denom = produce(kh + 1, static) if kh + 1 < N_KV_HEADS else None
            consume(kh, denom)
            denom = next_denom

    @pl.when(static_ok)
    def _():
        heads(True)

    @pl.when(jnp.logical_not(static_ok))
    def _():
        heads(False)


def _window_bias(tq):
    span = tq + 2 * WINDOW
    key_rel = jnp.arange(span)[:, None] - WINDOW
    q_rel = (jnp.arange(GROUP * tq) % tq)[None, :]
    return jnp.where(jnp.abs(key_rel - q_rel) <= WINDOW, 0.0, MASKED_SCORE).astype(F32)


def _window_attention(qt, k, vt, qn, kn, sink):
    b, _, t = qt.shape
    tq, w = TQ_WIN, WINDOW
    r = tq // w
    nblk = t // w
    n = GROUP * tq
    span = tq + 2 * w
    sink_rows = jnp.repeat((sink.astype(F32) * LOG2E).reshape(N_KV_HEADS, 1, GROUP), tq, axis=2)
    prev = lambda i: jnp.maximum(i * r - 1, 0)
    nxt = lambda i: jnp.minimum(i * r + r, nblk - 1)
    kern = functools.partial(_window_attn_kernel, tq=tq)
    return pl.pallas_call(
        kern,
        grid=(b, t // tq),
        in_specs=[pl.BlockSpec((1, Q_DIM, tq), lambda bi, i: (bi, 0, i)),
                  pl.BlockSpec((1, N_KV_HEADS, w, K_AUG), lambda bi, i: (bi, 0, prev(i), 0)),
                  pl.BlockSpec((1, N_KV_HEADS, tq, K_AUG), lambda bi, i: (bi, 0, i, 0)),
                  pl.BlockSpec((1, N_KV_HEADS, w, K_AUG), lambda bi, i: (bi, 0, nxt(i), 0)),
                  pl.BlockSpec((1, 1, KV_DIM, w), lambda bi, i: (bi, prev(i), 0, 0)),
                  pl.BlockSpec((1, r, KV_DIM, w), lambda bi, i: (bi, i, 0, 0)),
                  pl.BlockSpec((1, 1, KV_DIM, w), lambda bi, i: (bi, nxt(i), 0, 0)),
                  pl.BlockSpec((1, N_HEADS, tq), lambda bi, i: (bi, 0, i)),
                  pl.BlockSpec((1, N_HEADS, t), lambda bi, i: (bi, 0, 0)),
                  pl.BlockSpec((1, KNORM_ROWS, t), lambda bi, i: (bi, 0, 0)),
                  _resident((span, n)),
                  _resident((N_KV_HEADS, 1, n))],
        out_specs=pl.BlockSpec((1, tq, Q_DIM), lambda bi, i: (bi, i, 0)),
        out_shape=jax.ShapeDtypeStruct((b, t, Q_DIM), BF16),
        scratch_shapes=[pltpu.VMEM((N_KV_HEADS, K_AUG, n), BF16),
                        pltpu.VMEM((2, span, n), BF16)] + _batch_check_scratch(),
        compiler_params=_params("parallel", "arbitrary"),
        name="attn_window",
    )(qt, k, k, k, vt, vt, vt, qn, qn, kn, _window_bias(tq), sink_rows)


def _signed_tables(ang):
    cos = jnp.cos(ang)
    sin = jnp.sin(ang)
    return jnp.concatenate([cos, cos], axis=1), jnp.concatenate([-sin, sin], axis=1)


def _window_tables(t):
    half = HEAD_DIM // 2
    inv_freq = ROPE_THETA ** (-jnp.arange(half, dtype=F32) / half)
    pos = jnp.arange(t, dtype=F32)
    cos, sin = _signed_tables(pos[:, None] * inv_freq[None, :])
    return cos.T, sin.T


def _axial_tables(t):
    half = HEAD_DIM // 4
    rows = t // GRID_W
    inv_freq = ROPE_THETA ** (-jnp.arange(half, dtype=F32) / half)
    row = jnp.repeat(jnp.arange(rows, dtype=F32), GRID_W)
    col = jnp.tile(jnp.arange(GRID_W, dtype=F32), rows)
    cos_r, sin_r = _signed_tables(row[:, None] * inv_freq[None, :])
    cos_c, sin_c = _signed_tables(col[:, None] * inv_freq[None, :])
    return (jnp.concatenate([cos_r, cos_c], axis=1).T,
            jnp.concatenate([sin_r, sin_c], axis=1).T)


def _encoder(x, p):
    b, t, d = x.shape
    tables = (_window_tables(t), _axial_tables(t))
    x = x.reshape(b * t, d)
    for layer in range(DEPTH):
        x = _ffn(x, p, layer)
        axial = layer % 2 == 1
        cos_t, sin_t = tables[layer % 2]
        if axial:
            qt, k, vt, qn, kn = _qkv(x.reshape(b, t, d), p, layer, cos_t, sin_t, axial=True, vch=TK_DENSE)
            a = _dense_attention(qt, k, vt, qn, kn)
        else:
            qt, k, vt, qn, kn = _qkv(x.reshape(b, t, d), p, layer, cos_t, sin_t, axial=False, vch=WINDOW)
            a = _window_attention(qt, k, vt, qn, kn, p["attn_sink"][layer // 2])
        x = _mixer_out_ffn(a.reshape(b * t, Q_DIM), x, p, layer)
    return x.reshape(b, t, d)


def kernel(x_prompt, x_sample, norm_g, w_qkv, w_o, attn_sink, q_norm, k_norm, w_gate, w_up, w_down):
    p = {"norm_g": norm_g.reshape(DEPTH, -1, 1, D_MODEL),
         "q_norm": q_norm.reshape(-1, HEAD_DIM, 1), "k_norm": k_norm.reshape(-1, HEAD_DIM, 1),
         "attn_sink": attn_sink,
         "w_qkv": w_qkv.astype(BF16), "w_o": w_o.astype(BF16),
         "w_gate": w_gate.astype(BF16), "w_up": w_up.astype(BF16), "w_down": w_down.astype(BF16)}
    return (_encoder(x_prompt, p), _encoder(x_sample, p))
```
